```python
import math
import jax, jax.numpy as jnp
from jax import lax
import numpy as np

D_MODEL = 1024
BATCH = 4
SEQ = 8192
DEPTH = 2
DEC_BATCH = 32
DEC_SEQ = 4
PAST_LEN = 16384
PAGE_SIZE = 128

N_HEADS = D_MODEL // 128
HEAD_DIM = 64
V_DIM = 2 * HEAD_DIM
ROT_DIM = HEAD_DIM // 4
ROPE_THETA = 500000.0
Q_BLOCK = 128
SUBLN_EPS = 1e-5
D_INNER = 2 * D_MODEL
SSM_HEAD_DIM = 64
SSM_HEADS = D_INNER // SSM_HEAD_DIM
SSM_GROUPS = 8
HEADS_PER_GROUP = SSM_HEADS // SSM_GROUPS
D_STATE = 128
CONV_K = 4
CONV_DIM = D_INNER + 2 * SSM_GROUPS * D_STATE
IN_DIM = D_INNER + CONV_DIM + SSM_HEADS
CHUNK = 128
D_FF = 4 * D_MODEL
PLE_DIM = 256
EPS = 1e-6
NEG_INF = -1e30

kernel_name = "hybrid_diffattn_mamba2_decoder_step"


def rmsnorm(x, w, eps=EPS):
    xf = x.astype(jnp.float32)
    y = xf * lax.rsqrt(jnp.mean(xf * xf, axis=-1, keepdims=True) + eps)
    return (y * w.astype(jnp.float32)).astype(x.dtype)


def rope_partial(x, pos):
    half = ROT_DIM // 2
    inv_freq = jnp.power(ROPE_THETA, -jnp.arange(half, dtype=jnp.float32) * (2.0 / ROT_DIM))
    ang = pos.astype(jnp.float32)[:, None] * inv_freq[None, :]
    cos = jnp.cos(ang)[:, None, :]
    sin = jnp.sin(ang)[:, None, :]
    xf = x.astype(jnp.float32)
    x1, x2 = xf[..., :half], xf[..., half:ROT_DIM]
    out = jnp.concatenate([x1 * cos - x2 * sin, x2 * cos + x1 * sin, xf[..., ROT_DIM:]], axis=-1)
    return out.astype(x.dtype)


def diff_lambda(lq1, lk1, lq2, lk2, lam_init):
    f = lambda a: a.astype(jnp.float32)
    return jnp.exp(jnp.sum(f(lq1) * f(lk1))) - jnp.exp(jnp.sum(f(lq2) * f(lk2))) + lam_init


def attn_qkv(hn, pos, w_qkv):
    b, t, _ = hn.shape
    qkv = hn @ w_qkv
    q = qkv[..., :D_MODEL].reshape(b, t, 2 * N_HEADS, HEAD_DIM)
    k = qkv[..., D_MODEL:2 * D_MODEL].reshape(b, t, 2 * N_HEADS, HEAD_DIM)
    v = qkv[..., 2 * D_MODEL:].reshape(b, t, N_HEADS, V_DIM)
    return rope_partial(q, pos), rope_partial(k, pos), v


def diff_attend(q, k, v, mask, lam):
    b, tq = q.shape[:2]
    tk = k.shape[1]
    s = jnp.einsum('bqhd,bkhd->bhqk', q, k, preferred_element_type=jnp.float32) * (HEAD_DIM ** -0.5)
    s = jnp.where(mask, s, NEG_INF)
    a = jax.nn.softmax(s, axis=-1).reshape(b, N_HEADS, 2, tq, tk)
    w = a[:, :, 0] - lam * a[:, :, 1]
    return jnp.einsum('bhqk,bkhe->bqhe', w.astype(v.dtype), v)


def prompt_attention(q, k, v, lam):
    b, t = q.shape[:2]
    nb = t // Q_BLOCK
    qb = jnp.moveaxis(q.reshape(b, nb, Q_BLOCK, 2 * N_HEADS, HEAD_DIM), 1, 0)
    starts = jnp.arange(nb, dtype=jnp.int32) * Q_BLOCK
    kpos = jnp.arange(t, dtype=jnp.int32)

    def block(args):
        qblk, st = args
        qpos = st + jnp.arange(Q_BLOCK, dtype=jnp.int32)
        return diff_attend(qblk, k, v, kpos[None, :] <= qpos[:, None], lam)

    o = lax.map(block, (qb, starts))
    return jnp.moveaxis(o, 0, 1).reshape(b, t, N_HEADS, V_DIM)


def sample_attention(q, k, v, cache_k, cache_v, layer, page_table, lam):
    t = q.shape[1]
    past = page_table.shape[1] * cache_k.shape[2]
    kpos = jnp.concatenate([jnp.arange(past, dtype=jnp.int32), past + jnp.arange(t, dtype=jnp.int32)])
    qpos = past + jnp.arange(t, dtype=jnp.int32)
    mask = kpos[None, :] <= qpos[:, None]

    def one(args):
        qs, kn, vn, pt = args
        kpast = cache_k[layer, pt].reshape(past, 2 * N_HEADS, HEAD_DIM).astype(kn.dtype)
        vpast = cache_v[layer, pt].reshape(past, N_HEADS, V_DIM).astype(vn.dtype)
        kk = jnp.concatenate([kpast, kn], axis=0)
        vv = jnp.concatenate([vpast, vn], axis=0)
        return diff_attend(qs[None], kk[None], vv[None], mask, lam)[0]

    return lax.map(one, (q, k, v, page_table))


def attn_out(o, subln_w, w_o, lam_init):
    b, t = o.shape[:2]
    o = rmsnorm(o, subln_w, SUBLN_EPS) * (1.0 - lam_init)
    return o.reshape(b, t, D_MODEL) @ w_o


def ssd_scan(x, dt, A, B, C, h0):
    b, t = x.shape[:2]
    L = CHUNK if t % CHUNK == 0 else t
    nc = t // L
    f32 = jnp.float32
    chunks = lambda a: jnp.moveaxis(a.astype(f32).reshape((b, nc, L) + a.shape[2:]), 1, 0)
    causal = jnp.tril(jnp.ones((L, L), dtype=bool))[None, :, :, None, None]

    def step(h, inp):
        xc, dtc, Bc, Cc = inp
        acs = jnp.cumsum(dtc * A, axis=1)
        seg = acs[:, :, None] - acs[:, None, :]
        decay = jnp.exp(jnp.where(causal, seg, -jnp.inf))
        cb = jnp.einsum('bign,bjgn->bgij', Cc, Bc)
        w = jnp.einsum('bgij,bijgr->bijgr', cb, decay) * dtc[:, None]
        y = jnp.einsum('bijgr,bjgrp->bigrp', w, xc)
        y = y + jnp.einsum('bign,bgrpn->bigrp', Cc, h) * jnp.exp(acs)[..., None]
        last = acs[:, -1]
        h_new = h * jnp.exp(last)[..., None, None] + jnp.einsum(
            'bjgn,bjgr,bjgrp->bgrpn', Bc, jnp.exp(last[:, None] - acs) * dtc, xc)
        return h_new, y

    hT, ys = lax.scan(step, h0.astype(f32), (chunks(x), chunks(dt), chunks(B), chunks(C)))
    y = jnp.moveaxis(ys, 0, 1).reshape(x.shape)
    return y, hT


def ssm_mixer(hn, ssm_state, conv_state, w_in, conv_w, conv_b, dt_bias, a_log, d_skip, gnorm_w, w_out):
    b, t, _ = hn.shape
    proj = hn @ w_in
    z = proj[..., :D_INNER]
    xbc = proj[..., D_INNER:D_INNER + CONV_DIM]
    dt_raw = proj[..., D_INNER + CONV_DIM:]
    xpad = jnp.concatenate([conv_state.astype(xbc.dtype), xbc], axis=1)
    new_conv = xpad[:, xpad.shape[1] - (CONV_K - 1):]
    acc = conv_b
    for j in range(CONV_K):
        acc = acc + xpad[:, j:j + t] * conv_w[j]
    xbc = jax.nn.silu(acc)
    xs = xbc[..., :D_INNER].reshape(b, t, SSM_GROUPS, HEADS_PER_GROUP, SSM_HEAD_DIM)
    Bm = xbc[..., D_INNER:D_INNER + SSM_GROUPS * D_STATE].reshape(b, t, SSM_GROUPS, D_STATE)
    Cm = xbc[..., D_INNER + SSM_GROUPS * D_STATE:].reshape(b, t, SSM_GROUPS, D_STATE)
    dt = jax.nn.softplus((dt_raw + dt_bias).astype(jnp.float32)).reshape(b, t, SSM_GROUPS, HEADS_PER_GROUP)
    A = -jnp.exp(a_log.astype(jnp.float32)).reshape(SSM_GROUPS, HEADS_PER_GROUP)
    h0 = ssm_state.reshape(b, SSM_GROUPS, HEADS_PER_GROUP, SSM_HEAD_DIM, D_STATE)
    y, hT = ssd_scan(xs, dt, A, Bm, Cm, h0)
    y = y + d_skip.astype(jnp.float32).reshape(SSM_GROUPS, HEADS_PER_GROUP)[..., None] * xs.astype(jnp.float32)
    y = y.astype(hn.dtype).reshape(b, t, D_INNER) * jax.nn.silu(z)
    y = rmsnorm(y.reshape(b, t, SSM_GROUPS, D_INNER // SSM_GROUPS),
                gnorm_w.reshape(SSM_GROUPS, D_INNER // SSM_GROUPS)).reshape(b, t, D_INNER)
    new_state = hT.reshape(b, SSM_HEADS, SSM_HEAD_DIM, D_STATE).astype(ssm_state.dtype)
    return y @ w_out, new_state, new_conv


def mlp(hn, w_up, w_down):
    a = jax.nn.relu(hn @ w_up)
    return (a * a) @ w_down


def ple_add(h, p, norm_w, w_gate, w_proj):
    gate = jax.nn.sigmoid(rmsnorm(h, norm_w) @ w_gate)
    return h + gate * (p @ w_proj)


def setup_inputs(seed: int = 0) -> dict:
    key = jax.random.key(seed)
    ks = jax.random.split(key, 40)
    f32 = jnp.float32
    n_pages = PAST_LEN // PAGE_SIZE
    n_used = DEC_BATCH * n_pages
    n_pool = n_used + n_used // 4
    na = (DEPTH + 1) // 2
    ns = DEPTH // 2
    nrm = lambda k, shape, scale: scale * jax.random.normal(k, shape, f32)
    gain = lambda k, shape: 1.0 + 0.05 * jax.random.normal(k, shape, f32)
    page_table = jax.random.permutation(ks[7], n_pool)[:n_used].reshape(DEC_BATCH, n_pages).astype(jnp.int32)
    dt0 = jnp.exp(jax.random.uniform(ks[20], (ns, SSM_HEADS), f32) * (math.log(0.1) - math.log(0.001)) + math.log(0.001))
    dt_bias = dt0 + jnp.log(-jnp.expm1(-dt0))
    a_log = jnp.log(jax.random.uniform(ks[21], (ns, SSM_HEADS), f32, 1.0, 16.0))
    return {
        "x_prompt": nrm(ks[0], (BATCH, SEQ, D_MODEL), 1.0),
        "x_sample": nrm(ks[1], (DEC_BATCH, DEC_SEQ, D_MODEL), 1.0),
        "cache_k": nrm(ks[2], (na, n_pool, PAGE_SIZE, 2 * N_HEADS, HEAD_DIM), 1.0),
        "cache_v": nrm(ks[3], (na, n_pool, PAGE_SIZE, N_HEADS, V_DIM), 1.0),
        "state_ssm": nrm(ks[4], (ns, DEC_BATCH, SSM_HEADS, SSM_HEAD_DIM, D_STATE), 0.5),
        "state_conv": nrm(ks[5], (ns, DEC_BATCH, CONV_K - 1, CONV_DIM), 1.0),
        "page_table": page_table,
        "p_prompt": nrm(ks[8], (DEPTH, BATCH, SEQ, PLE_DIM), 1.0),
        "p_sample": nrm(ks[9], (DEPTH, DEC_BATCH, DEC_SEQ, PLE_DIM), 1.0),
        "attn_norm": gain(ks[10], (na, D_MODEL)),
        "w_qkv": nrm(ks[11], (na, D_MODEL, 3 * D_MODEL), D_MODEL ** -0.5),
        "lambda_q1": nrm(ks[12], (na, HEAD_DIM), 0.1),
        "lambda_k1": nrm(ks[13], (na, HEAD_DIM), 0.1),
        "lambda_q2": nrm(ks[14], (na, HEAD_DIM), 0.1),
        "lambda_k2": nrm(ks[15], (na, HEAD_DIM), 0.1),
        "subln_w": gain(ks[16], (na, V_DIM)),
        "w_o": nrm(ks[17], (na, D_MODEL, D_MODEL), D_MODEL ** -0.5),
        "ssm_norm": gain(ks[18], (ns, D_MODEL)),
        "w_in": nrm(ks[19], (ns, D_MODEL, IN_DIM), D_MODEL ** -0.5),
        "conv_w": nrm(ks[22], (ns, CONV_K, CONV_DIM), CONV_K ** -0.5),
        "conv_b": nrm(ks[23], (ns, CONV_DIM), 0.02),
        "dt_bias": dt_bias,
        "a_log": a_log,
        "d_skip": 1.0 + 0.1 * jax.random.normal(ks[24], (ns, SSM_HEADS), f32),
        "gnorm_w": gain(ks[25], (ns, D_INNER)),
        "w_out": nrm(ks[26], (ns, D_INNER, D_MODEL), D_INNER ** -0.5),
        "mlp_norm": gain(ks[27], (DEPTH, D_MODEL)),
        "w_up": nrm(ks[28], (DEPTH, D_MODEL, D_FF), D_MODEL ** -0.5),
        "w_down": nrm(ks[29], (DEPTH, D_FF, D_MODEL), D_FF ** -0.5),
        "ple_norm": gain(ks[30], (DEPTH, D_MODEL)),
        "w_ple_gate": nrm(ks[31], (DEPTH, D_MODEL, D_MODEL), D_MODEL ** -0.5),
        "w_ple_proj": nrm(ks[32], (DEPTH, PLE_DIM, D_MODEL), PLE_DIM ** -0.5),
        "final_norm": gain(ks[33], (D_MODEL,)),
    }


def reference(x_prompt, x_sample, cache_k, cache_v, state_ssm, state_conv, page_table, p_prompt, p_sample,
              attn_norm, w_qkv, lambda_q1, lambda_k1, lambda_q2, lambda_k2, subln_w, w_o,
              ssm_norm, w_in, conv_w, conv_b, dt_bias, a_log, d_skip, gnorm_w, w_out,
              mlp_norm, w_up, w_down, ple_norm, w_ple_gate, w_ple_proj, final_norm):
    past_len = page_table.shape[1] * cache_k.shape[2]
    pos_p = jnp.arange(x_prompt.shape[1], dtype=jnp.int32)
    pos_s = past_len + jnp.arange(x_sample.shape[1], dtype=jnp.int32)
    hp, hs = x_prompt, x_sample
    kp_l, vp_l, ks_l, vs_l = [], [], [], []
    sp_l, cp_l, ss_l, cs_l = [], [], [], []
    for i in range(DEPTH):
        if i % 2 == 0:
            a = i // 2
            lam_init = 0.8 - 0.6 * math.exp(-0.3 * i)
            lam = diff_lambda(lambda_q1[a], lambda_k1[a], lambda_q2[a], lambda_k2[a], lam_init)
            q, k, v = attn_qkv(rmsnorm(hp, attn_norm[a]), pos_p, w_qkv[a])
            hp = hp + attn_out(prompt_attention(q, k, v, lam), subln_w[a], w_o[a], lam_init)
            kp_l.append(k)
            vp_l.append(v)
            q, k, v = attn_qkv(rmsnorm(hs, attn_norm[a]), pos_s, w_qkv[a])
            o = sample_attention(q, k, v, cache_k, cache_v, a, page_table, lam)
            hs = hs + attn_out(o, subln_w[a], w_o[a], lam_init)
            ks_l.append(k)
            vs_l.append(v)
        else:
            s = i // 2
            wts = (w_in[s], conv_w[s], conv_b[s], dt_bias[s], a_log[s], d_skip[s], gnorm_w[s], w_out[s])
            bp = hp.shape[0]
            h0 = jnp.zeros((bp, SSM_HEADS, SSM_HEAD_DIM, D_STATE), jnp.float32)
            c0 = jnp.zeros((bp, CONV_K - 1, CONV_DIM), hp.dtype)
            y, st, cv = ssm_mixer(rmsnorm(hp, ssm_norm[s]), h0, c0, *wts)
            hp = hp + y
            sp_l.append(st)
            cp_l.append(cv)
            y, st, cv = ssm_mixer(rmsnorm(hs, ssm_norm[s]), state_ssm[s], state_conv[s], *wts)
            hs = hs + y
            ss_l.append(st)
            cs_l.append(cv)
        hp = hp + mlp(rmsnorm(hp, mlp_norm[i]), w_up[i], w_down[i])
        hs = hs + mlp(rmsnorm(hs, mlp_norm[i]), w_up[i], w_down[i])
        hp = ple_add(hp, p_prompt[i], ple_norm[i], w_ple_gate[i], w_ple_proj[i])
        hs = ple_add(hs, p_sample[i], ple_norm[i], w_ple_gate[i], w_ple_proj[i])
    y_prompt = rmsnorm(hp, final_norm)
    y_sample = rmsnorm(hs, final_norm)
    k_prompt = jnp.stack(kp_l)
    v_prompt = jnp.stack(vp_l)
    k_sample = jnp.stack(ks_l)
    v_sample = jnp.stack(vs_l)
    ssm_prompt = jnp.stack(sp_l)
    conv_prompt = jnp.stack(cp_l)
    ssm_sample = jnp.stack(ss_l)
    conv_sample = jnp.stack(cs_l)
    return (y_prompt, y_sample, k_prompt, v_prompt, k_sample, v_sample, ssm_prompt, conv_prompt, ssm_sample, conv_sample)
```

```python
import functools
import math

import jax
import jax.numpy as jnp
from jax import lax
from jax.experimental import pallas as pl
from jax.experimental.pallas import tpu as pltpu

F32 = jnp.float32
BF16 = jnp.bfloat16

HEAD_DIM = 64
V_DIM = 2 * HEAD_DIM
ROT_DIM = HEAD_DIM // 4
ROPE_THETA = 500000.0
SUBLN_EPS = 1e-5
EPS = 1e-6
NEG_INF = -1e30
SSM_HEAD_DIM = 64
SSM_GROUPS = 8
D_STATE = 128
CONV_K = 4
CHUNK = 128
LANES = 128
SUBLANES = 8
VMEM_LIMIT = 56 * 1024 * 1024


def _params(*sem):
    return pltpu.CompilerParams(dimension_semantics=sem, vmem_limit_bytes=VMEM_LIMIT)


def _resident(shape):
    return pl.BlockSpec(shape, lambda *_: (0,) * len(shape), pipeline_mode=pl.Buffered(1))


def _rms(x, g, eps):
    ms = jnp.mean(x * x, axis=-1, keepdims=True)
    return x * lax.rsqrt(ms + eps) * g


def _row_to_col(row):
    n = row.shape[1]
    eye = lax.broadcasted_iota(jnp.int32, (n, n), 0) == lax.broadcasted_iota(jnp.int32, (n, n), 1)
    return jnp.sum(jnp.where(eye, jnp.broadcast_to(row, (n, n)), 0.0), axis=1, keepdims=True)


def _diff_lambda(lq1, lk1, lq2, lk2, lam_init):
    s1 = jnp.sum(lq1[...] * lk1[...], axis=1, keepdims=True)
    s2 = jnp.sum(lq2[...] * lk2[...], axis=1, keepdims=True)
    return jnp.exp(s1) - jnp.exp(s2) + lam_init


def _qkv_kernel(x_ref, g_ref, w_ref, c_ref, s1_ref, s2_ref, q_ref, k_ref, v_ref):
    d = x_ref.shape[1]
    xn = _rms(x_ref[...], g_ref[...], EPS).astype(BF16)
    c, s1, s2 = c_ref[...], s1_ref[...], s2_ref[...]
    for part, out in ((0, q_ref), (1, k_ref)):
        y = jnp.dot(xn, w_ref[:, part * d:(part + 1) * d], preferred_element_type=F32)
        for hb in range(d // LANES):
            blk = y[:, hb * LANES:(hb + 1) * LANES]
            out[:, hb * LANES:(hb + 1) * LANES] = (
                blk * c + pltpu.roll(blk, LANES - ROT_DIM // 2, 1) * s1 + pltpu.roll(blk, ROT_DIM // 2, 1) * s2)
    v_ref[...] = jnp.dot(xn, w_ref[:, 2 * d:3 * d], preferred_element_type=F32)


def _rope_tables(pos):
    half = ROT_DIM // 2
    inv_freq = jnp.power(ROPE_THETA, -jnp.arange(half, dtype=F32) * (2.0 / ROT_DIM))
    ang = pos.astype(F32)[:, None] * inv_freq[None, :]
    cos, sin = jnp.cos(ang), jnp.sin(ang)
    t = pos.shape[0]
    z8 = jnp.zeros((t, half), F32)
    rest0 = jnp.zeros((t, HEAD_DIM - ROT_DIM), F32)
    c = jnp.concatenate([cos, cos, jnp.ones((t, HEAD_DIM - ROT_DIM), F32)], axis=1)
    s1 = jnp.concatenate([-sin, z8, rest0], axis=1)
    s2 = jnp.concatenate([z8, sin, rest0], axis=1)
    rep = LANES // HEAD_DIM
    return tuple(jnp.tile(a, (1, rep)) for a in (c, s1, s2))


def _qkv(x, gain, w, tabs, tm):
    m, d = x.shape
    nt = tabs[0].shape[0] // tm
    row = pl.BlockSpec((tm, d), lambda i: (i, 0))
    tab = pl.BlockSpec((tm, LANES), lambda i: (i % nt, 0))
    return pl.pallas_call(
        _qkv_kernel,
        grid=(m // tm,),
        in_specs=[row, _resident((1, d)), _resident((d, 3 * d)), tab, tab, tab],
        out_specs=[row, row, row],
        out_shape=[jax.ShapeDtypeStruct((m, d), F32)] * 3,
        compiler_params=_params("parallel"),
        name="qkv",
    )(x, gain, w, *tabs)


def _attn_kernel(q_ref, k_ref, v_ref, lq1, lk1, lq2, lk2, sw_ref, o_ref, kb, vtb, m_s, l_s, acc_s,
                 *, tq, lam_init):
    qi = pl.program_id(2)
    nk = vtb.shape[0]

    @pl.when(qi == 0)
    def _():
        for c in range(nk):
            kb[c * tq:(c + 1) * tq, :] = k_ref[0, c * tq:(c + 1) * tq, :].astype(BF16)
            vtb[c] = v_ref[0, c * tq:(c + 1) * tq, :].T.astype(BF16)

    qt = (q_ref[0] * (HEAD_DIM ** -0.5)).T
    rowi = lax.broadcasted_iota(jnp.int32, qt.shape, 0)
    qbd = jnp.concatenate([jnp.where(rowi < HEAD_DIM, qt, 0.0), jnp.where(rowi >= HEAD_DIM, qt, 0.0)],
                          axis=1).astype(BF16)
    m_s[...] = jnp.full(m_s.shape, NEG_INF, F32)
    l_s[...] = jnp.zeros(l_s.shape, F32)
    acc_s[...] = jnp.zeros(acc_s.shape, F32)

    def step(ki, masked):
        start = pl.multiple_of(ki * tq, tq)
        st = jnp.dot(kb[pl.ds(start, tq), :], qbd, preferred_element_type=F32)
        if masked:
            kj = lax.broadcasted_iota(jnp.int32, st.shape, 0)
            qc = lax.broadcasted_iota(jnp.int32, st.shape, 1)
            qc = jnp.where(qc >= tq, qc - tq, qc)
            st = jnp.where(kj <= qc, st, NEG_INF)
        m_prev = m_s[...]
        m_new = jnp.maximum(m_prev, jnp.max(st, axis=0, keepdims=True))
        alpha = jnp.exp(m_prev - m_new)
        p = jnp.exp(st - m_new)
        l_s[...] = alpha * l_s[...] + jnp.sum(p, axis=0, keepdims=True)
        acc_s[...] = alpha * acc_s[...] + jnp.dot(vtb[ki], p.astype(BF16), preferred_element_type=F32)
        m_s[...] = m_new

    def body(ki, carry):
        step(ki, False)
        return carry

    lax.fori_loop(0, qi, body, 0)
    step(qi, True)

    a = acc_s[...] * (1.0 / l_s[...])
    lam = _diff_lambda(lq1, lk1, lq2, lk2, lam_init)
    ot = a[:, :tq] - lam * a[:, tq:]
    ms = jnp.mean(ot * ot, axis=0, keepdims=True)
    ot = ot * lax.rsqrt(ms + SUBLN_EPS) * sw_ref[...] * (1.0 - lam_init)
    o_ref[0] = ot.T.astype(o_ref.dtype)


def _prompt_attention(q, k, v, lam_params, subln_col, lam_init, tq):
    b, t, d = q.shape
    hp = d // V_DIM
    qspec = pl.BlockSpec((1, tq, V_DIM), lambda bi, h, qi: (bi, qi, h))
    kvspec = pl.BlockSpec((1, t, V_DIM), lambda bi, h, qi: (bi, 0, h))
    lspec = _resident((1, HEAD_DIM))
    return pl.pallas_call(
        functools.partial(_attn_kernel, tq=tq, lam_init=lam_init),
        grid=(b, hp, t // tq),
        in_specs=[qspec, kvspec, kvspec, lspec, lspec, lspec, lspec, _resident((V_DIM, 1))],
        out_specs=qspec,
        out_shape=jax.ShapeDtypeStruct((b, t, d), BF16),
        scratch_shapes=[pltpu.VMEM((t, V_DIM), BF16), pltpu.VMEM((t // tq, V_DIM, tq), BF16),
                        pltpu.VMEM((1, 2 * tq), F32), pltpu.VMEM((1, 2 * tq), F32),
                        pltpu.VMEM((V_DIM, 2 * tq), F32)],
        compiler_params=_params("parallel", "parallel", "arbitrary"),
        name="prompt_attn",
    )(q, k, v, *lam_params, subln_col)


def _sattn_kernel(pt_ref, q_ref, kn_ref, vn_ref, *rest, pps, lam_init, n_new):
    del pt_ref
    k_refs, v_refs = rest[:pps], rest[pps:2 * pps]
    lq1, lk1, lq2, lk2, sw_ref, o_ref, qbd_s, m_s, l_s, acc_s = rest[2 * pps:]
    g = pl.program_id(1)
    ng = pl.num_programs(1)
    d = q_ref.shape[2]
    nc = d // HEAD_DIM
    ncol = nc * SUBLANES

    @pl.when(g == 0)
    def _():
        q = q_ref[0] * (HEAD_DIM ** -0.5)
        qt = jnp.concatenate([q] * nc, axis=0)
        r = lax.broadcasted_iota(jnp.int32, qt.shape, 0) // SUBLANES
        cc = lax.broadcasted_iota(jnp.int32, qt.shape, 1) // HEAD_DIM
        qbd_s[...] = jnp.where(r == cc, qt, 0.0).T.astype(BF16)
        m_s[...] = jnp.full(m_s.shape, NEG_INF, F32)
        l_s[...] = jnp.zeros(l_s.shape, F32)
        acc_s[...] = jnp.zeros(acc_s.shape, F32)

    def update(kv_pairs, mask):
        qbd = qbd_s[...]
        scores = []
        for kr, _ in kv_pairs:
            s = jnp.dot(kr[0].astype(BF16), qbd, preferred_element_type=F32)
            if mask is not None:
                s = jnp.where(mask, s, NEG_INF)
            scores.append(s)
        m_prev = m_s[...]
        m_new = m_prev
        for s in scores:
            m_new = jnp.maximum(m_new, jnp.max(s, axis=0, keepdims=True))
        alpha = jnp.exp(m_prev - m_new)
        lsum = jnp.zeros_like(m_prev)
        pv = jnp.zeros(acc_s.shape, F32)
        for s, (_, vr) in zip(scores, kv_pairs):
            p = jnp.exp(s - m_new)
            lsum = lsum + jnp.sum(p, axis=0, keepdims=True)
            pv = pv + jnp.dot(p.T.astype(BF16), vr[0].astype(BF16), preferred_element_type=F32)
        acc_s[...] = _row_to_col(alpha) * acc_s[...] + pv
        l_s[...] = alpha * l_s[...] + lsum
        m_s[...] = m_new

    update(list(zip(k_refs, v_refs)), None)

    @pl.when(g == ng - 1)
    def _():
        page = kn_ref.shape[1]
        kj = lax.broadcasted_iota(jnp.int32, (page, ncol), 0)
        slot = lax.broadcasted_iota(jnp.int32, (page, ncol), 1) % SUBLANES
        mask = (kj < n_new) & (kj <= slot)
        update([(kn_ref, vn_ref)], mask)
        a = acc_s[...] * _row_to_col(1.0 / l_s[...])
        lam = _diff_lambda(lq1, lk1, lq2, lk2, lam_init)
        for h in range(d // V_DIM):
            cols = slice(h * V_DIM, (h + 1) * V_DIM)
            a1 = a[(2 * h) * SUBLANES:(2 * h + 1) * SUBLANES, cols]
            a2 = a[(2 * h + 1) * SUBLANES:(2 * h + 2) * SUBLANES, cols]
            o = a1 - lam * a2
            o_ref[0, :, cols] = _rms(o, sw_ref[...], SUBLN_EPS) * (1.0 - lam_init)


def _sample_attention(q, k_new, v_new, cache_k, cache_v, layer, page_table, lam_params, subln_row, lam_init, pps):
    s, n_new, d = q.shape
    page = cache_k.shape[2]
    n_pages = page_table.shape[1]
    ncol = (d // HEAD_DIM) * SUBLANES
    qp = jnp.pad(q, ((0, 0), (0, SUBLANES - n_new), (0, 0)))
    knp = jnp.pad(k_new, ((0, 0), (0, page - n_new), (0, 0)))
    vnp = jnp.pad(v_new, ((0, 0), (0, page - n_new), (0, 0)))
    ck = cache_k.reshape(cache_k.shape[0], cache_k.shape[1], page, d)
    cv = cache_v.reshape(cache_v.shape[0], cache_v.shape[1], page, d)

    def page_spec(j):
        return pl.BlockSpec((1, 1, page, d), lambda si, g, pt: (layer, pt[si, g * pps + j], 0, 0))

    seq8 = pl.BlockSpec((1, SUBLANES, d), lambda si, g, pt: (si, 0, 0))
    seqp = pl.BlockSpec((1, page, d), lambda si, g, pt: (si, 0, 0))
    small = lambda shape: pl.BlockSpec(shape, lambda si, g, pt: (0,) * len(shape))
    kernel = functools.partial(_sattn_kernel, pps=pps, lam_init=lam_init, n_new=n_new)

    def body(pt_ref, q_ref, kn_ref, vn_ref, *rest):
        pages = [r.at[0] for r in rest[:2 * pps]]
        kernel(pt_ref, q_ref, kn_ref, vn_ref, *pages, *rest[2 * pps:])

    out = pl.pallas_call(
        body,
        grid_spec=pltpu.PrefetchScalarGridSpec(
            num_scalar_prefetch=1,
            grid=(s, n_pages // pps),
            in_specs=[seq8, seqp, seqp] + [page_spec(j) for j in range(pps)] * 2
            + [small((1, HEAD_DIM))] * 4 + [small((1, V_DIM))],
            out_specs=seq8,
            scratch_shapes=[pltpu.VMEM((d, ncol), BF16), pltpu.VMEM((1, ncol), F32), pltpu.VMEM((1, ncol), F32),
                            pltpu.VMEM((ncol, d), F32)],
        ),
        out_shape=jax.ShapeDtypeStruct((s, SUBLANES, d), F32),
        compiler_params=_params("parallel", "arbitrary"),
        name="sample_attn",
    )(page_table, qp, knp, vnp, *([ck] * pps), *([cv] * pps), *lam_params, subln_row)
    return out[:, :n_new]


def _oproj_kernel(o_ref, h_ref, w_ref, out_ref):
    out_ref[...] = h_ref[...] + jnp.dot(o_ref[...].astype(BF16), w_ref[...], preferred_element_type=F32)


def _oproj(o, h, w, tm):
    m, d = h.shape
    row = pl.BlockSpec((tm, d), lambda i: (i, 0))
    return pl.pallas_call(
        _oproj_kernel, grid=(m // tm,),
        in_specs=[row, row, _resident((d, d))], out_specs=row,
        out_shape=jax.ShapeDtypeStruct((m, d), F32),
        compiler_params=_params("parallel"), name="oproj",
    )(o, h, w)


def _mlp_kernel(h_ref, g_ref, wu_ref, wd_ref, out_ref, *, fc):
    h = h_ref[...]
    xn = _rms(h, g_ref[...], EPS).astype(BF16)
    acc = h
    for c in range(wu_ref.shape[1] // fc):
        a = jnp.maximum(jnp.dot(xn, wu_ref[:, c * fc:(c + 1) * fc], preferred_element_type=F32), 0.0)
        acc = acc + jnp.dot((a * a).astype(BF16), wd_ref[c * fc:(c + 1) * fc, :], preferred_element_type=F32)
    out_ref[...] = acc


def _mlp(h, gain, wu, wd, tm):
    m, d = h.shape
    ff = wu.shape[1]
    row = pl.BlockSpec((tm, d), lambda i: (i, 0))
    return pl.pallas_call(
        functools.partial(_mlp_kernel, fc=min(ff, 1024)), grid=(m // tm,),
        in_specs=[row, _resident((1, d)), _resident((d, ff)), _resident((ff, d))], out_specs=row,
        out_shape=jax.ShapeDtypeStruct((m, d), F32),
        compiler_params=_params("parallel"), name="mlp",
    )(h, gain, wu, wd)


def _ple_kernel(h_ref, p_ref, g_ref, wg_ref, wp_ref, *rest, final):
    h = h_ref[...]
    gate = jax.nn.sigmoid(jnp.dot(_rms(h, g_ref[...], EPS).astype(BF16), wg_ref[...], preferred_element_type=F32))
    out = h + gate * jnp.dot(p_ref[...].astype(BF16), wp_ref[...], preferred_element_type=F32)
    if final:
        fn_ref, out_ref, y_ref = rest
        y_ref[...] = _rms(out, fn_ref[...], EPS)
    else:
        (out_ref,) = rest
    out_ref[...] = out


def _ple(h, p, gain, wg, wp, final_gain, tm):
    m, d = h.shape
    pd = p.shape[1]
    row = pl.BlockSpec((tm, d), lambda i: (i, 0))
    prow = pl.BlockSpec((tm, pd), lambda i: (i, 0))
    final = final_gain is not None
    ins = [h, p, gain, wg, wp] + ([final_gain] if final else [])
    in_specs = [row, prow, _resident((1, d)), _resident((d, d)), _resident((pd, d))] + ([_resident((1, d))] if final else [])
    n_out = 2 if final else 1
    outs = pl.pallas_call(
        functools.partial(_ple_kernel, final=final), grid=(m // tm,),
        in_specs=in_specs, out_specs=[row] * n_out,
        out_shape=[jax.ShapeDtypeStruct((m, d), F32)] * n_out,
        compiler_params=_params("parallel"), name="ple",
    )(*ins)
    return outs if final else outs[0]


def _inproj_kernel(x_ref, g_ref, w_ref, z_ref, xbc_ref, dt_ref):
    xn = _rms(x_ref[...], g_ref[...], EPS).astype(BF16)
    nz, nx = z_ref.shape[1], xbc_ref.shape[1]
    z_ref[...] = jnp.dot(xn, w_ref[:, :nz], preferred_element_type=F32)
    xbc_ref[...] = jnp.dot(xn, w_ref[:, nz:nz + nx], preferred_element_type=F32)
    dt_ref[...] = jnp.dot(xn, w_ref[:, nz + nx:], preferred_element_type=F32)


def _inproj(x, gain, w, d_inner, conv_dim, tm):
    m, d = x.shape
    n = w.shape[1]
    ndt = n - d_inner - conv_dim
    row = lambda width: pl.BlockSpec((tm, width), lambda i: (i, 0))
    return pl.pallas_call(
        _inproj_kernel, grid=(m // tm,),
        in_specs=[row(d), _resident((1, d)), _resident((d, n))],
        out_specs=[row(d_inner), row(conv_dim), row(ndt)],
        out_shape=[jax.ShapeDtypeStruct((m, d_inner), F32), jax.ShapeDtypeStruct((m, conv_dim), F32),
                   jax.ShapeDtypeStruct((m, ndt), F32)],
        compiler_params=_params("parallel"), name="inproj",
    )(x, gain, w)


def _outproj_kernel(y_ref, z_ref, h_ref, gw_ref, w_ref, out_ref):
    z = z_ref[...]
    yg = y_ref[...] * (z * jax.nn.sigmoid(z))
    gs = yg.shape[1] // SSM_GROUPS
    parts = []
    for g in range(SSM_GROUPS):
        cols = slice(g * gs, (g + 1) * gs)
        parts.append(_rms(yg[:, cols], gw_ref[:, cols], EPS).astype(BF16))
    out_ref[...] = h_ref[...] + jnp.dot(jnp.concatenate(parts, axis=1), w_ref[...], preferred_element_type=F32)


def _outproj(y, z, h, gw, w, tm):
    m, d = h.shape
    di = y.shape[1]
    row = pl.BlockSpec((tm, d), lambda i: (i, 0))
    wide = pl.BlockSpec((tm, di), lambda i: (i, 0))
    return pl.pallas_call(
        _outproj_kernel, grid=(m // tm,),
        in_specs=[wide, wide, row, _resident((1, di)), _resident((di, d))], out_specs=row,
        out_shape=jax.ShapeDtypeStruct((m, d), F32),
        compiler_params=_params("parallel"), name="outproj",
    )(y, z, h, gw, w)


def _softplus(x):
    return jnp.maximum(x, 0.0) + jnp.log1p(jnp.exp(-jnp.abs(x)))


def _ssd_kernel(xbc_ref, dt_ref, c0_ref, h0_ref, cw_ref, cb_ref, dtb_ref, alog_ref, dsk_ref,
                y_ref, hout_ref, ht_s, win_s, act_s, *, valid, n_heads):
    c = pl.program_id(1)
    nchunks = pl.num_programs(1)
    L = xbc_ref.shape[1]
    di = n_heads * SSM_HEAD_DIM
    hpg = n_heads // SSM_GROUPS
    gw = hpg * SSM_HEAD_DIM
    halo = SUBLANES

    @pl.when(c == 0)
    def _():
        ht_s[...] = h0_ref[0].T
        win_s[0:halo, :] = c0_ref[0]

    win_s[halo:halo + L, :] = xbc_ref[0]
    conv = cb_ref[...]
    for j in range(CONV_K):
        off = halo - (CONV_K - 1) + j
        conv = conv + cw_ref[j:j + 1, :] * win_s[off:off + L, :]
    win_s[0:halo, :] = win_s[L:L + halo, :]
    act_s[...] = conv * jax.nn.sigmoid(conv)

    dt = _softplus(dt_ref[0] + dtb_ref[...])
    lane = lax.broadcasted_iota(jnp.int32, dt.shape, 1)
    ok = lane < n_heads
    if valid < L:
        ok = ok & (lax.broadcasted_iota(jnp.int32, dt.shape, 0) < valid)
    dt = jnp.where(ok, dt, 0.0)
    dta = dt * (-jnp.exp(alog_ref[...]))
    tril = lax.broadcasted_iota(jnp.int32, (L, L), 0) >= lax.broadcasted_iota(jnp.int32, (L, L), 1)
    trilb = jnp.where(tril, 1.0, 0.0).astype(BF16)
    hi = dta.astype(BF16)
    r1 = dta - hi.astype(F32)
    mid = r1.astype(BF16)
    lo = (r1 - mid.astype(F32)).astype(BF16)
    acs = (jnp.dot(trilb, hi, preferred_element_type=F32) + jnp.dot(trilb, mid, preferred_element_type=F32)
           + jnp.dot(trilb, lo, preferred_element_type=F32))
    acs_t = acs.T
    dt_t = dt.T
    last = acs[L - 1:L, :]
    e_acs = jnp.exp(acs)
    s_in = jnp.exp(last - acs) * dt
    e_last = jnp.exp(last)

    for g in range(SSM_GROUPS):
        bcol = di + g * D_STATE
        ccol = di + SSM_GROUPS * D_STATE + g * D_STATE
        bg_t = act_s[:, bcol:bcol + D_STATE].T.astype(BF16)
        cg = act_s[:, ccol:ccol + D_STATE].astype(BF16)
        cbm = jnp.dot(cg, bg_t, preferred_element_type=F32)
        hg = ht_s[:, g * gw:(g + 1) * gw]
        yin = jnp.dot(cg, hg.astype(BF16), preferred_element_type=F32)
        xw, el = [], []
        for r in range(hpg):
            hd = g * hpg + r
            xcols = slice(hd * SSM_HEAD_DIM, (hd + 1) * SSM_HEAD_DIM)
            seg = acs[:, hd:hd + 1] - acs_t[hd:hd + 1, :]
            dec = jnp.exp(jnp.where(tril, seg, NEG_INF))
            w = (cbm * dec * dt_t[hd:hd + 1, :]).astype(BF16)
            x_r = act_s[:, xcols]
            y_ref[0, :, xcols] = (jnp.dot(w, x_r.astype(BF16), preferred_element_type=F32)
                                  + yin[:, r * SSM_HEAD_DIM:(r + 1) * SSM_HEAD_DIM] * e_acs[:, hd:hd + 1]
                                  + dsk_ref[:, xcols] * x_r)
            xw.append(x_r * s_in[:, hd:hd + 1])
            el.append(jnp.broadcast_to(e_last[:, hd:hd + 1], (1, SSM_HEAD_DIM)))
        xw = jnp.concatenate(xw, axis=1).astype(BF16) if hpg > 1 else xw[0].astype(BF16)
        el = jnp.concatenate(el, axis=1) if hpg > 1 else el[0]
        ht_s[:, g * gw:(g + 1) * gw] = hg * el + jnp.dot(bg_t, xw, preferred_element_type=F32)

    @pl.when(c == nchunks - 1)
    def _():
        hout_ref[0] = ht_s[...].T


def _ssd(xbc, dt_raw, c0, h0, cw, cb, dtb, alog, dsk, valid):
    b, t, cd = xbc.shape
    di = h0.shape[1]
    L = min(CHUNK, t)
    kernel = functools.partial(_ssd_kernel, valid=valid, n_heads=di // SSM_HEAD_DIM)
    chunk = lambda width: pl.BlockSpec((1, L, width), lambda bi, c: (bi, c, 0))
    per_b = lambda rows, width: pl.BlockSpec((1, rows, width), lambda bi, c: (bi, 0, 0))
    small = lambda shape: pl.BlockSpec(shape, lambda bi, c: (0,) * len(shape))
    return pl.pallas_call(
        kernel, grid=(b, t // L),
        in_specs=[chunk(cd), chunk(LANES), per_b(SUBLANES, cd), per_b(di, D_STATE),
                  small((CONV_K, cd)), small((1, cd)), small((1, LANES)), small((1, LANES)), small((1, di))],
        out_specs=[chunk(di), per_b(di, D_STATE)],
        out_shape=[jax.ShapeDtypeStruct((b, t, di), F32), jax.ShapeDtypeStruct((b, di, D_STATE), F32)],
        scratch_shapes=[pltpu.VMEM((D_STATE, di), F32), pltpu.VMEM((L + 2 * SUBLANES, cd), F32),
                        pltpu.VMEM((L, cd), F32)],
        compiler_params=_params("parallel", "arbitrary"), name="ssd",
    )(xbc, dt_raw, c0, h0, cw, cb, dtb, alog, dsk)


def _tile(m, want):
    return want if m % want == 0 else m


def kernel(x_prompt, x_sample, cache_k, cache_v, state_ssm, state_conv, page_table, p_prompt, p_sample, attn_norm, w_qkv, lambda_q1, lambda_k1, lambda_q2, lambda_k2, subln_w, w_o, ssm_norm, w_in, conv_w, conv_b, dt_bias, a_log, d_skip, gnorm_w, w_out, mlp_norm, w_up, w_down, ple_norm, w_ple_gate, w_ple_proj, final_norm):
    bp, tp, d = x_prompt.shape
    bs, ts, _ = x_sample.shape
    depth = mlp_norm.shape[0]
    past_len = page_table.shape[1] * cache_k.shape[2]
    n_heads_ssm = dt_bias.shape[1] if dt_bias.ndim == 2 else 0
    mp, msz = bp * tp, bs * ts
    tm_p = _tile(mp, 512)
    tm_s = msz
    bf = lambda a: a.astype(BF16)
    row = lambda a: a.reshape(1, -1)

    hp = x_prompt.reshape(mp, d)
    hs = x_sample.reshape(msz, d)
    tabs_p = _rope_tables(jnp.arange(tp, dtype=jnp.int32))
    tabs_s = _rope_tables(jnp.tile(past_len + jnp.arange(ts, dtype=jnp.int32), bs))

    kp_l, vp_l, ks_l, vs_l, sp_l, cp_l, ss_l, cs_l = [], [], [], [], [], [], [], []
    yp = ys = None
    for i in range(depth):
        if i % 2 == 0:
            a = i // 2
            lam_init = 0.8 - 0.6 * math.exp(-0.3 * i)
            lam_params = [row(lambda_q1[a]), row(lambda_k1[a]), row(lambda_q2[a]), row(lambda_k2[a])]
            wqkv, wo, gain = bf(w_qkv[a]), bf(w_o[a]), row(attn_norm[a])
            q, k, v = _qkv(hp, gain, wqkv, tabs_p, _tile(tp, 512))
            o = _prompt_attention(q.reshape(bp, tp, d), k.reshape(bp, tp, d), v.reshape(bp, tp, d), lam_params,
                                  subln_w[a].reshape(V_DIM, 1), lam_init, _tile(tp, 256))
            hp = _oproj(o.reshape(mp, d), hp, wo, tm_p)
            kp_l.append(k.reshape(bp, tp, d // HEAD_DIM, HEAD_DIM))
            vp_l.append(v.reshape(bp, tp, d // V_DIM, V_DIM))
            q, k, v = _qkv(hs, gain, wqkv, tabs_s, tm_s)
            o = _sample_attention(q.reshape(bs, ts, d), k.reshape(bs, ts, d), v.reshape(bs, ts, d), cache_k, cache_v,
                                  a, page_table, lam_params, row(subln_w[a]), lam_init, 8)
            hs = _oproj(o.reshape(msz, d), hs, wo, tm_s)
            ks_l.append(k.reshape(bs, ts, d // HEAD_DIM, HEAD_DIM))
            vs_l.append(v.reshape(bs, ts, d // V_DIM, V_DIM))
        else:
            s = i // 2
            nh = n_heads_ssm
            di = nh * SSM_HEAD_DIM
            cd = conv_w.shape[2]
            pad_h = LANES - nh
            win = bf(jnp.pad(w_in[s], ((0, 0), (0, pad_h))))
            dtb = jnp.pad(dt_bias[s], (0, pad_h)).reshape(1, LANES)
            alog = jnp.pad(a_log[s], (0, pad_h)).reshape(1, LANES)
            dsk = jnp.repeat(d_skip[s], SSM_HEAD_DIM).reshape(1, di)
            gain, gw, wout = row(ssm_norm[s]), row(gnorm_w[s]), bf(w_out[s])
            cw, cb = conv_w[s], row(conv_b[s])
            z, xbc, dtr = _inproj(hp, gain, win, di, cd, _tile(mp, 256))
            y, st = _ssd(xbc.reshape(bp, tp, cd), dtr.reshape(bp, tp, LANES), jnp.zeros((bp, SUBLANES, cd), F32),
                         jnp.zeros((bp, di, D_STATE), F32), cw, cb, dtb, alog, dsk, CHUNK)
            hp = _outproj(y.reshape(mp, di), z, hp, gw, wout, tm_p)
            sp_l.append(st.reshape(bp, nh, SSM_HEAD_DIM, D_STATE))
            cp_l.append(xbc.reshape(bp, tp, cd)[:, tp - (CONV_K - 1):])
            z, xbc, dtr = _inproj(hs, gain, win, di, cd, tm_s)
            xbc3 = xbc.reshape(bs, ts, cd)
            padt = ((0, 0), (0, CHUNK - ts), (0, 0))
            c0 = jnp.pad(state_conv[s], ((0, 0), (SUBLANES - (CONV_K - 1), 0), (0, 0)))
            y, st = _ssd(jnp.pad(xbc3, padt), jnp.pad(dtr.reshape(bs, ts, LANES), padt), c0,
                         state_ssm[s].reshape(bs, di, D_STATE), cw, cb, dtb, alog, dsk, ts)
            hs = _outproj(y[:, :ts].reshape(msz, di), z, hs, gw, wout, tm_s)
            ss_l.append(st.reshape(bs, nh, SSM_HEAD_DIM, D_STATE))
            cs_l.append(jnp.concatenate([state_conv[s], xbc3], axis=1)[:, ts:])
        gain, wu, wd = row(mlp_norm[i]), bf(w_up[i]), bf(w_down[i])
        hp = _mlp(hp, gain, wu, wd, tm_p)
        hs = _mlp(hs, gain, wu, wd, tm_s)
        gain, wg, wp = row(ple_norm[i]), bf(w_ple_gate[i]), bf(w_ple_proj[i])
        fin = row(final_norm) if i == depth - 1 else None
        rp = _ple(hp, p_prompt[i].reshape(mp, -1), gain, wg, wp, fin, tm_p)
        rs = _ple(hs, p_sample[i].reshape(msz, -1), gain, wg, wp, fin, tm_s)
        if fin is None:
            hp, hs = rp, rs
        else:
            (hp, yp), (hs, ys) = rp, rs

    return (yp.reshape(bp, tp, d), ys.reshape(bs, ts, d), jnp.stack(kp_l), jnp.stack(vp_l), jnp.stack(ks_l),
            jnp.stack(vs_l), jnp.stack(sp_l), jnp.stack(cp_l), jnp.stack(ss_l), jnp.stack(cs_l))
```

```python
import functools
import math

import jax
import jax.numpy as jnp
from jax import lax
from jax.experimental import pallas as pl
from jax.experimental.pallas import tpu as pltpu

F32 = jnp.float32
BF16 = jnp.bfloat16

HEAD_DIM = 64
V_DIM = 2 * HEAD_DIM
ROT_DIM = HEAD_DIM // 4
ROPE_THETA = 500000.0
SUBLN_EPS = 1e-5
EPS = 1e-6
NEG_INF = -1e30
LOG2E = 1.4426950408889634
SSM_HEAD_DIM = 64
SSM_GROUPS = 8
D_STATE = 128
CONV_K = 4
CHUNK = 128
LANES = 128
SUBLANES = 8
VMEM_LIMIT = 56 * 1024 * 1024


def _params(*sem):
    return pltpu.CompilerParams(dimension_semantics=sem, vmem_limit_bytes=VMEM_LIMIT)


def _resident(shape):
    return pl.BlockSpec(shape, lambda *_: (0,) * len(shape), pipeline_mode=pl.Buffered(1))


def _rms(x, g, eps):
    ms = jnp.mean(x * x, axis=-1, keepdims=True)
    return x * lax.rsqrt(ms + eps) * g


def _row_to_col(row):
    n = row.shape[1]
    eye = lax.broadcasted_iota(jnp.int32, (n, n), 0) == lax.broadcasted_iota(jnp.int32, (n, n), 1)
    return jnp.sum(jnp.where(eye, jnp.broadcast_to(row, (n, n)), 0.0), axis=1, keepdims=True)


def _diff_lambda(lq1, lk1, lq2, lk2, lam_init):
    s1 = jnp.sum(lq1[...] * lk1[...], axis=1, keepdims=True)
    s2 = jnp.sum(lq2[...] * lk2[...], axis=1, keepdims=True)
    return jnp.exp(s1) - jnp.exp(s2) + lam_init


def _qkv_kernel(x_ref, g_ref, w_ref, c_ref, s1_ref, s2_ref, q_ref, k_ref, v_ref):
    d = x_ref.shape[1]
    xn = _rms(x_ref[...], g_ref[...], EPS).astype(BF16)
    c, s1, s2 = c_ref[...], s1_ref[...], s2_ref[...]
    for part, out in ((0, q_ref), (1, k_ref)):
        y = jnp.dot(xn, w_ref[:, part * d:(part + 1) * d], preferred_element_type=F32)
        for hb in range(d // LANES):
            blk = y[:, hb * LANES:(hb + 1) * LANES]
            out[:, hb * LANES:(hb + 1) * LANES] = (
                blk * c + pltpu.roll(blk, LANES - ROT_DIM // 2, 1) * s1 + pltpu.roll(blk, ROT_DIM // 2, 1) * s2)
    v_ref[...] = jnp.dot(xn, w_ref[:, 2 * d:3 * d], preferred_element_type=F32)


def _rope_tables(pos):
    half = ROT_DIM // 2
    inv_freq = jnp.power(ROPE_THETA, -jnp.arange(half, dtype=F32) * (2.0 / ROT_DIM))
    ang = pos.astype(F32)[:, None] * inv_freq[None, :]
    cos, sin = jnp.cos(ang), jnp.sin(ang)
    t = pos.shape[0]
    z8 = jnp.zeros((t, half), F32)
    rest0 = jnp.zeros((t, HEAD_DIM - ROT_DIM), F32)
    c = jnp.concatenate([cos, cos, jnp.ones((t, HEAD_DIM - ROT_DIM), F32)], axis=1)
    s1 = jnp.concatenate([-sin, z8, rest0], axis=1)
    s2 = jnp.concatenate([z8, sin, rest0], axis=1)
    rep = LANES // HEAD_DIM
    return tuple(jnp.tile(a, (1, rep)) for a in (c, s1, s2))


def _qkv(x, gain, w, tabs, tm):
    m, d = x.shape
    nt = tabs[0].shape[0] // tm
    row = pl.BlockSpec((tm, d), lambda i: (i, 0))
    tab = pl.BlockSpec((tm, LANES), lambda i: (i % nt, 0))
    return pl.pallas_call(
        _qkv_kernel,
        grid=(m // tm,),
        in_specs=[row, _resident((1, d)), _resident((d, 3 * d)), tab, tab, tab],
        out_specs=[row, row, row],
        out_shape=[jax.ShapeDtypeStruct((m, d), F32)] * 3,
        compiler_params=_params("parallel"),
        name="qkv",
    )(x, gain, w, *tabs)


def _attn_kernel(q_ref, k_ref, v_ref, lq1, lk1, lq2, lk2, sw_ref, o_ref,
                 kb, vtb, qbd_s, sa, sb, mxa, mxb, m_s, l_s, acc_s, *, tq, lam_init):
    qi = pl.program_id(2)
    nk = vtb.shape[0]

    @pl.when(qi == 0)
    def _():
        for c in range(nk):
            kb[c * tq:(c + 1) * tq, :] = k_ref[0, c * tq:(c + 1) * tq, :].astype(BF16)
            vtb[c] = v_ref[0, c * tq:(c + 1) * tq, :].T.astype(BF16)

    qt = (q_ref[0] * (HEAD_DIM ** -0.5 * LOG2E)).T
    rowi = lax.broadcasted_iota(jnp.int32, qt.shape, 0)
    qbd_s[...] = jnp.concatenate([jnp.where(rowi < HEAD_DIM, qt, 0.0), jnp.where(rowi >= HEAD_DIM, qt, 0.0)],
                                 axis=1).astype(BF16)
    m_s[...] = jnp.full(m_s.shape, NEG_INF, F32)
    l_s[...] = jnp.zeros(l_s.shape, F32)
    acc_s[...] = jnp.zeros(acc_s.shape, F32)

    def scores(blk, s_ref, mx_ref, masked):
        start = pl.multiple_of(blk * tq, tq)
        st = jnp.dot(kb[pl.ds(start, tq), :], qbd_s[...], preferred_element_type=F32)
        if masked:
            kj = lax.broadcasted_iota(jnp.int32, st.shape, 0)
            qc = lax.broadcasted_iota(jnp.int32, st.shape, 1)
            qc = jnp.where(qc >= tq, qc - tq, qc)
            st = jnp.where(kj <= qc, st, NEG_INF)
        s_ref[...] = st
        mx_ref[...] = jnp.max(st, axis=0, keepdims=True)

    def consume(blk, s_ref, mx_ref):
        m_prev = m_s[...]
        m_new = jnp.maximum(m_prev, mx_ref[...])
        alpha = jnp.exp2(m_prev - m_new)
        p = jnp.exp2(s_ref[...] - m_new)
        l_s[...] = alpha * l_s[...] + jnp.sum(p, axis=0, keepdims=True)
        acc_s[...] = alpha * acc_s[...] + jnp.dot(vtb[blk], p.astype(BF16), preferred_element_type=F32)
        m_s[...] = m_new

    scores(qi, sa, mxa, True)
    npair = qi // 2

    def body(jj, carry):
        scores(2 * jj, sb, mxb, False)
        consume(jnp.where(jj == 0, qi, 2 * jj - 1), sa, mxa)
        scores(2 * jj + 1, sa, mxa, False)
        consume(2 * jj, sb, mxb)
        return carry

    lax.fori_loop(0, npair, body, 0)
    blk_a = jnp.where(npair == 0, qi, 2 * npair - 1)

    @pl.when(qi % 2 == 0)
    def _():
        consume(blk_a, sa, mxa)

    @pl.when(qi % 2 == 1)
    def _():
        scores(qi - 1, sb, mxb, False)
        consume(blk_a, sa, mxa)
        consume(qi - 1, sb, mxb)

    a = acc_s[...] * (1.0 / l_s[...])
    lam = _diff_lambda(lq1, lk1, lq2, lk2, lam_init)
    ot = a[:, :tq] - lam * a[:, tq:]
    ms = jnp.mean(ot * ot, axis=0, keepdims=True)
    ot = ot * lax.rsqrt(ms + SUBLN_EPS) * sw_ref[...] * (1.0 - lam_init)
    o_ref[0] = ot.T.astype(o_ref.dtype)


def _prompt_attention(q, k, v, lam_params, subln_col, lam_init, tq):
    b, t, d = q.shape
    hp = d // V_DIM
    qspec = pl.BlockSpec((1, tq, V_DIM), lambda bi, h, qi: (bi, qi, h))
    kvspec = pl.BlockSpec((1, t, V_DIM), lambda bi, h, qi: (bi, 0, h))
    lspec = _resident((1, HEAD_DIM))
    return pl.pallas_call(
        functools.partial(_attn_kernel, tq=tq, lam_init=lam_init),
        grid=(b, hp, t // tq),
        in_specs=[qspec, kvspec, kvspec, lspec, lspec, lspec, lspec, _resident((V_DIM, 1))],
        out_specs=qspec,
        out_shape=jax.ShapeDtypeStruct((b, t, d), BF16),
        scratch_shapes=[pltpu.VMEM((t, V_DIM), BF16), pltpu.VMEM((t // tq, V_DIM, tq), BF16),
                        pltpu.VMEM((V_DIM, 2 * tq), BF16),
                        pltpu.VMEM((tq, 2 * tq), F32), pltpu.VMEM((tq, 2 * tq), F32),
                        pltpu.VMEM((1, 2 * tq), F32), pltpu.VMEM((1, 2 * tq), F32),
                        pltpu.VMEM((1, 2 * tq), F32), pltpu.VMEM((1, 2 * tq), F32),
                        pltpu.VMEM((V_DIM, 2 * tq), F32)],
        compiler_params=_params("parallel", "parallel", "arbitrary"),
        name="prompt_attn",
    )(q, k, v, *lam_params, subln_col)


def _sattn_kernel(pt_ref, q_ref, kn_ref, vn_ref, *rest, pps, lam_init, n_new):
    del pt_ref
    k_refs, v_refs = rest[:pps], rest[pps:2 * pps]
    lq1, lk1, lq2, lk2, sw_ref, o_ref, m_s, l_s, acc_s = rest[2 * pps:]
    g = pl.program_id(1)
    ng = pl.num_programs(1)
    d = q_ref.shape[2]
    nc = d // HEAD_DIM
    nh = d // V_DIM
    page = kn_ref.shape[3]
    rows = nc * SUBLANES

    @pl.when(g == 0)
    def _():
        m_s[...] = jnp.full(m_s.shape, NEG_INF, F32)
        l_s[...] = jnp.zeros(l_s.shape, F32)
        acc_s[...] = jnp.zeros(acc_s.shape, F32)

    q = (q_ref[0] * (HEAD_DIM ** -0.5)).astype(BF16)
    qcs = [q[:, c * HEAD_DIM:(c + 1) * HEAD_DIM] for c in range(nc)]

    def update(kv_pairs, mask):
        scores = []
        for kr, _ in kv_pairs:
            s = jnp.concatenate([jnp.dot(qcs[c], kr[c].astype(BF16), preferred_element_type=F32) for c in range(nc)],
                                axis=0)
            if mask is not None:
                s = jnp.where(mask, s, NEG_INF)
            scores.append(s)
        mx = scores[0]
        for s in scores[1:]:
            mx = jnp.maximum(mx, s)
        m_prev = m_s[...]
        m_new = jnp.maximum(m_prev, jnp.max(mx, axis=1, keepdims=True))
        alpha = jnp.exp(m_prev - m_new)
        psum = None
        pv = [None] * nh
        for s, (_, vr) in zip(scores, kv_pairs):
            p = jnp.exp(s - m_new)
            psum = p if psum is None else psum + p
            pb = p.astype(BF16)
            for h in range(nh):
                vh = vr[pl.ds(h, page, stride=nh), :].astype(BF16)
                t = jnp.dot(pb[2 * h * SUBLANES:(2 * h + 2) * SUBLANES, :], vh, preferred_element_type=F32)
                pv[h] = t if pv[h] is None else pv[h] + t
        acc_s[...] = alpha * acc_s[...] + jnp.concatenate(pv, axis=0)
        l_s[...] = alpha * l_s[...] + jnp.sum(psum, axis=1, keepdims=True)
        m_s[...] = m_new

    update(list(zip(k_refs, v_refs)), None)

    @pl.when(g == ng - 1)
    def _():
        kj = lax.broadcasted_iota(jnp.int32, (rows, page), 1)
        slot = lax.broadcasted_iota(jnp.int32, (rows, page), 0) % SUBLANES
        mask = (kj < n_new) & (kj <= slot)
        update([(kn_ref.at[0], vn_ref.at[0])], mask)
        a = acc_s[...] * (1.0 / l_s[...])
        lam = _diff_lambda(lq1, lk1, lq2, lk2, lam_init)
        for h in range(nh):
            a1 = a[(2 * h) * SUBLANES:(2 * h + 1) * SUBLANES, :]
            a2 = a[(2 * h + 1) * SUBLANES:(2 * h + 2) * SUBLANES, :]
            o = a1 - lam * a2
            o_ref[0, :, h * V_DIM:(h + 1) * V_DIM] = _rms(o, sw_ref[...], SUBLN_EPS) * (1.0 - lam_init)


def _sample_attention(q, k_new, v_new, cache_k, cache_v, layer, page_table, lam_params, subln_row, lam_init, pps):
    s, n_new, d = q.shape
    na, n_pool, page, nc, _ = cache_k.shape
    nh = cache_v.shape[3]
    n_pages = page_table.shape[1]
    rows = nc * SUBLANES
    qp = jnp.pad(q, ((0, 0), (0, SUBLANES - n_new), (0, 0)))
    ck = jnp.transpose(cache_k, (0, 1, 3, 4, 2))
    cv = cache_v.reshape(na, n_pool, page * nh, V_DIM)
    knp = jnp.pad(jnp.transpose(k_new.reshape(s, n_new, nc, HEAD_DIM), (0, 2, 3, 1)),
                  ((0, 0), (0, 0), (0, 0), (0, page - n_new)))
    vnp = jnp.pad(v_new, ((0, 0), (0, page - n_new), (0, 0))).reshape(s, page * nh, V_DIM)

    kspec = lambda j: pl.BlockSpec((None, None, nc, HEAD_DIM, page),
                                   lambda si, g, pt: (layer, pt[si, g * pps + j], 0, 0, 0))
    vspec = lambda j: pl.BlockSpec((None, None, page * nh, V_DIM), lambda si, g, pt: (layer, pt[si, g * pps + j], 0, 0))
    seq8 = pl.BlockSpec((1, SUBLANES, d), lambda si, g, pt: (si, 0, 0))
    knspec = pl.BlockSpec((1, nc, HEAD_DIM, page), lambda si, g, pt: (si, 0, 0, 0))
    vnspec = pl.BlockSpec((1, page * nh, V_DIM), lambda si, g, pt: (si, 0, 0))
    small = lambda shape: pl.BlockSpec(shape, lambda si, g, pt: (0,) * len(shape))

    out = pl.pallas_call(
        functools.partial(_sattn_kernel, pps=pps, lam_init=lam_init, n_new=n_new),
        grid_spec=pltpu.PrefetchScalarGridSpec(
            num_scalar_prefetch=1,
            grid=(s, n_pages // pps),
            in_specs=[seq8, knspec, vnspec] + [kspec(j) for j in range(pps)] + [vspec(j) for j in range(pps)]
            + [small((1, HEAD_DIM))] * 4 + [small((1, V_DIM))],
            out_specs=seq8,
            scratch_shapes=[pltpu.VMEM((rows, 1), F32), pltpu.VMEM((rows, 1), F32), pltpu.VMEM((rows, V_DIM), F32)],
        ),
        out_shape=jax.ShapeDtypeStruct((s, SUBLANES, d), F32),
        compiler_params=_params("parallel", "arbitrary"),
        name="sample_attn",
    )(page_table, qp, knp, vnp, *([ck] * pps), *([cv] * pps), *lam_params, subln_row)
    return out[:, :n_new]


def _oproj_kernel(o_ref, h_ref, w_ref, out_ref):
    out_ref[...] = h_ref[...] + jnp.dot(o_ref[...].astype(BF16), w_ref[...], preferred_element_type=F32)


def _oproj(o, h, w, tm):
    m, d = h.shape
    row = pl.BlockSpec((tm, d), lambda i: (i, 0))
    return pl.pallas_call(
        _oproj_kernel, grid=(m // tm,),
        in_specs=[row, row, _resident((d, d))], out_specs=row,
        out_shape=jax.ShapeDtypeStruct((m, d), F32),
        compiler_params=_params("parallel"), name="oproj",
    )(o, h, w)


def _mlp_kernel(h_ref, g_ref, wu_ref, wd_ref, out_ref, *, fc):
    h = h_ref[...]
    xn = _rms(h, g_ref[...], EPS).astype(BF16)
    acc = h
    for c in range(wu_ref.shape[1] // fc):
        a = jnp.maximum(jnp.dot(xn, wu_ref[:, c * fc:(c + 1) * fc], preferred_element_type=F32), 0.0)
        acc = acc + jnp.dot((a * a).astype(BF16), wd_ref[c * fc:(c + 1) * fc, :], preferred_element_type=F32)
    out_ref[...] = acc


def _mlp(h, gain, wu, wd, tm):
    m, d = h.shape
    ff = wu.shape[1]
    row = pl.BlockSpec((tm, d), lambda i: (i, 0))
    return pl.pallas_call(
        functools.partial(_mlp_kernel, fc=min(ff, 1024)), grid=(m // tm,),
        in_specs=[row, _resident((1, d)), _resident((d, ff)), _resident((ff, d))], out_specs=row,
        out_shape=jax.ShapeDtypeStruct((m, d), F32),
        compiler_params=_params("parallel"), name="mlp",
    )(h, gain, wu, wd)


def _ple_kernel(h_ref, p_ref, g_ref, wg_ref, wp_ref, *rest, final):
    h = h_ref[...]
    gate = jax.nn.sigmoid(jnp.dot(_rms(h, g_ref[...], EPS).astype(BF16), wg_ref[...], preferred_element_type=F32))
    out = h + gate * jnp.dot(p_ref[...].astype(BF16), wp_ref[...], preferred_element_type=F32)
    if final:
        fn_ref, out_ref, y_ref = rest
        y_ref[...] = _rms(out, fn_ref[...], EPS)
    else:
        (out_ref,) = rest
    out_ref[...] = out


def _ple(h, p, gain, wg, wp, final_gain, tm):
    m, d = h.shape
    pd = p.shape[1]
    row = pl.BlockSpec((tm, d), lambda i: (i, 0))
    prow = pl.BlockSpec((tm, pd), lambda i: (i, 0))
    final = final_gain is not None
    ins = [h, p, gain, wg, wp] + ([final_gain] if final else [])
    in_specs = [row, prow, _resident((1, d)), _resident((d, d)), _resident((pd, d))] + ([_resident((1, d))] if final else [])
    n_out = 2 if final else 1
    outs = pl.pallas_call(
        functools.partial(_ple_kernel, final=final), grid=(m // tm,),
        in_specs=in_specs, out_specs=[row] * n_out,
        out_shape=[jax.ShapeDtypeStruct((m, d), F32)] * n_out,
        compiler_params=_params("parallel"), name="ple",
    )(*ins)
    return outs if final else outs[0]


def _inproj_kernel(x_ref, g_ref, w_ref, z_ref, xbc_ref, dt_ref):
    xn = _rms(x_ref[...], g_ref[...], EPS).astype(BF16)
    nz, nx = z_ref.shape[1], xbc_ref.shape[1]
    z_ref[...] = jnp.dot(xn, w_ref[:, :nz], preferred_element_type=F32)
    xbc_ref[...] = jnp.dot(xn, w_ref[:, nz:nz + nx], preferred_element_type=F32)
    dt_ref[...] = jnp.dot(xn, w_ref[:, nz + nx:], preferred_element_type=F32)


def _inproj(x, gain, w, d_inner, conv_dim, tm):
    m, d = x.shape
    n = w.shape[1]
    ndt = n - d_inner - conv_dim
    row = lambda width: pl.BlockSpec((tm, width), lambda i: (i, 0))
    return pl.pallas_call(
        _inproj_kernel, grid=(m // tm,),
        in_specs=[row(d), _resident((1, d)), _resident((d, n))],
        out_specs=[row(d_inner), row(conv_dim), row(ndt)],
        out_shape=[jax.ShapeDtypeStruct((m, d_inner), F32), jax.ShapeDtypeStruct((m, conv_dim), F32),
                   jax.ShapeDtypeStruct((m, ndt), F32)],
        compiler_params=_params("parallel"), name="inproj",
    )(x, gain, w)


def _outproj_kernel(y_ref, z_ref, h_ref, gw_ref, w_ref, out_ref):
    z = z_ref[...]
    yg = y_ref[...] * (z * jax.nn.sigmoid(z))
    gs = yg.shape[1] // SSM_GROUPS
    parts = []
    for g in range(SSM_GROUPS):
        cols = slice(g * gs, (g + 1) * gs)
        parts.append(_rms(yg[:, cols], gw_ref[:, cols], EPS).astype(BF16))
    out_ref[...] = h_ref[...] + jnp.dot(jnp.concatenate(parts, axis=1), w_ref[...], preferred_element_type=F32)


def _outproj(y, z, h, gw, w, tm):
    m, d = h.shape
    di = y.shape[1]
    row = pl.BlockSpec((tm, d), lambda i: (i, 0))
    wide = pl.BlockSpec((tm, di), lambda i: (i, 0))
    return pl.pallas_call(
        _outproj_kernel, grid=(m // tm,),
        in_specs=[wide, wide, row, _resident((1, di)), _resident((di, d))], out_specs=row,
        out_shape=jax.ShapeDtypeStruct((m, d), F32),
        compiler_params=_params("parallel"), name="outproj",
    )(y, z, h, gw, w)


def _softplus(x):
    return jnp.maximum(x, 0.0) + jnp.log1p(jnp.exp(-jnp.abs(x)))


def _ssd_kernel(xbc_ref, dt_ref, c0_ref, h0_ref, cw_ref, cb_ref, dtb_ref, alog_ref, dsk_ref,
                y_ref, hout_ref, ht_s, win_s, act_s, *, valid, n_heads):
    c = pl.program_id(1)
    nchunks = pl.num_programs(1)
    L = xbc_ref.shape[1]
    di = n_heads * SSM_HEAD_DIM
    hpg = n_heads // SSM_GROUPS
    gw = hpg * SSM_HEAD_DIM
    halo = SUBLANES

    @pl.when(c == 0)
    def _():
        ht_s[...] = h0_ref[0].T
        win_s[0:halo, :] = c0_ref[0]

    win_s[halo:halo + L, :] = xbc_ref[0]
    conv = cb_ref[...]
    for j in range(CONV_K):
        off = halo - (CONV_K - 1) + j
        conv = conv + cw_ref[j:j + 1, :] * win_s[off:off + L, :]
    win_s[0:halo, :] = win_s[L:L + halo, :]
    act_s[...] = conv * jax.nn.sigmoid(conv)

    dt = _softplus(dt_ref[0] + dtb_ref[...])
    lane = lax.broadcasted_iota(jnp.int32, dt.shape, 1)
    ok = lane < n_heads
    if valid < L:
        ok = ok & (lax.broadcasted_iota(jnp.int32, dt.shape, 0) < valid)
    dt = jnp.where(ok, dt, 0.0)
    dta = dt * (-jnp.exp(alog_ref[...]))
    tril = lax.broadcasted_iota(jnp.int32, (L, L), 0) >= lax.broadcasted_iota(jnp.int32, (L, L), 1)
    trilb = jnp.where(tril, 1.0, 0.0).astype(BF16)
    hi = dta.astype(BF16)
    r1 = dta - hi.astype(F32)
    mid = r1.astype(BF16)
    lo = (r1 - mid.astype(F32)).astype(BF16)
    acs = (jnp.dot(trilb, hi, preferred_element_type=F32) + jnp.dot(trilb, mid, preferred_element_type=F32)
           + jnp.dot(trilb, lo, preferred_element_type=F32))
    acs_t = acs.T
    last = acs[L - 1:L, :]
    pw = 2 * SSM_HEAD_DIM
    lo = lax.broadcasted_iota(jnp.int32, (L, pw), 1) < SSM_HEAD_DIM

    def pair(a, h0, rows):
        return jnp.where(lo[:rows], jnp.broadcast_to(a[:, h0:h0 + 1], (rows, pw)),
                         jnp.broadcast_to(a[:, h0 + 1:h0 + 2], (rows, pw)))

    for g in range(SSM_GROUPS):
        bcol = di + g * D_STATE
        ccol = di + SSM_GROUPS * D_STATE + g * D_STATE
        bg_t = act_s[:, bcol:bcol + D_STATE].T.astype(BF16)
        cg = act_s[:, ccol:ccol + D_STATE].astype(BF16)
        cbm = jnp.dot(cg, bg_t, preferred_element_type=F32)
        hg = ht_s[:, g * gw:(g + 1) * gw]
        yin = jnp.dot(cg, hg.astype(BF16), preferred_element_type=F32)
        xw, el = [], []
        for pr in range(hpg // 2):
            h0 = g * hpg + 2 * pr
            cols = slice(h0 * SSM_HEAD_DIM, (h0 + 2) * SSM_HEAD_DIM)
            x2 = act_s[:, cols]
            acol = pair(acs, h0, L)
            last2 = pair(last, h0, 1)
            xdt = x2 * pair(dt, h0, L)
            xdtb = xdt.astype(BF16)
            ys = []
            for r in range(2):
                col = jnp.broadcast_to(acs[:, h0 + r:h0 + r + 1], (L, L))
                dec = jnp.exp(jnp.where(tril, col - acs_t[h0 + r:h0 + r + 1, :], NEG_INF))
                ys.append(jnp.dot((cbm * dec).astype(BF16), xdtb, preferred_element_type=F32))
            y_ref[0, :, cols] = (jnp.where(lo, ys[0], ys[1]) + yin[:, pr * pw:(pr + 1) * pw] * jnp.exp(acol)
                                 + dsk_ref[:, cols] * x2)
            xw.append((xdt * jnp.exp(last2 - acol)).astype(BF16))
            el.append(jnp.exp(last2))
        xw = jnp.concatenate(xw, axis=1) if len(xw) > 1 else xw[0]
        el = jnp.concatenate(el, axis=1) if len(el) > 1 else el[0]
        ht_s[:, g * gw:(g + 1) * gw] = hg * el + jnp.dot(bg_t, xw, preferred_element_type=F32)

    @pl.when(c == nchunks - 1)
    def _():
        hout_ref[0] = ht_s[...].T


def _ssd(xbc, dt_raw, c0, h0, cw, cb, dtb, alog, dsk, valid):
    b, t, cd = xbc.shape
    di = h0.shape[1]
    L = min(CHUNK, t)
    kernel = functools.partial(_ssd_kernel, valid=valid, n_heads=di // SSM_HEAD_DIM)
    chunk = lambda width: pl.BlockSpec((1, L, width), lambda bi, c: (bi, c, 0))
    per_b = lambda rows, width: pl.BlockSpec((1, rows, width), lambda bi, c: (bi, 0, 0))
    small = lambda shape: pl.BlockSpec(shape, lambda bi, c: (0,) * len(shape))
    return pl.pallas_call(
        kernel, grid=(b, t // L),
        in_specs=[chunk(cd), chunk(LANES), per_b(SUBLANES, cd), per_b(di, D_STATE),
                  small((CONV_K, cd)), small((1, cd)), small((1, LANES)), small((1, LANES)), small((1, di))],
        out_specs=[chunk(di), per_b(di, D_STATE)],
        out_shape=[jax.ShapeDtypeStruct((b, t, di), F32), jax.ShapeDtypeStruct((b, di, D_STATE), F32)],
        scratch_shapes=[pltpu.VMEM((D_STATE, di), F32), pltpu.VMEM((L + 2 * SUBLANES, cd), F32),
                        pltpu.VMEM((L, cd), F32)],
        compiler_params=_params("parallel", "arbitrary"), name="ssd",
    )(xbc, dt_raw, c0, h0, cw, cb, dtb, alog, dsk)


def _tile(m, want):
    return want if m % want == 0 else m


def kernel(x_prompt, x_sample, cache_k, cache_v, state_ssm, state_conv, page_table, p_prompt, p_sample, attn_norm, w_qkv, lambda_q1, lambda_k1, lambda_q2, lambda_k2, subln_w, w_o, ssm_norm, w_in, conv_w, conv_b, dt_bias, a_log, d_skip, gnorm_w, w_out, mlp_norm, w_up, w_down, ple_norm, w_ple_gate, w_ple_proj, final_norm):
    bp, tp, d = x_prompt.shape
    bs, ts, _ = x_sample.shape
    depth = mlp_norm.shape[0]
    past_len = page_table.shape[1] * cache_k.shape[2]
    n_heads_ssm = dt_bias.shape[1] if dt_bias.ndim == 2 else 0
    mp, msz = bp * tp, bs * ts
    tm_p = _tile(mp, 512)
    tm_s = msz
    bf = lambda a: a.astype(BF16)
    row = lambda a: a.reshape(1, -1)

    hp = x_prompt.reshape(mp, d)
    hs = x_sample.reshape(msz, d)
    tabs_p = _rope_tables(jnp.arange(tp, dtype=jnp.int32))
    tabs_s = _rope_tables(jnp.tile(past_len + jnp.arange(ts, dtype=jnp.int32), bs))

    kp_l, vp_l, ks_l, vs_l, sp_l, cp_l, ss_l, cs_l = [], [], [], [], [], [], [], []
    yp = ys = None
    for i in range(depth):
        if i % 2 == 0:
            a = i // 2
            lam_init = 0.8 - 0.6 * math.exp(-0.3 * i)
            lam_params = [row(lambda_q1[a]), row(lambda_k1[a]), row(lambda_q2[a]), row(lambda_k2[a])]
            wqkv, wo, gain = bf(w_qkv[a]), bf(w_o[a]), row(attn_norm[a])
            q, k, v = _qkv(hp, gain, wqkv, tabs_p, _tile(tp, 512))
            o = _prompt_attention(q.reshape(bp, tp, d), k.reshape(bp, tp, d), v.reshape(bp, tp, d), lam_params,
                                  subln_w[a].reshape(V_DIM, 1), lam_init, _tile(tp, 512))
            hp = _oproj(o.reshape(mp, d), hp, wo, tm_p)
            kp_l.append(k.reshape(bp, tp, d // HEAD_DIM, HEAD_DIM))
            vp_l.append(v.reshape(bp, tp, d // V_DIM, V_DIM))
            q, k, v = _qkv(hs, gain, wqkv, tabs_s, tm_s)
            o = _sample_attention(q.reshape(bs, ts, d), k.reshape(bs, ts, d), v.reshape(bs, ts, d), cache_k, cache_v,
                                  a, page_table, lam_params, row(subln_w[a]), lam_init, 8)
            hs = _oproj(o.reshape(msz, d), hs, wo, tm_s)
            ks_l.append(k.reshape(bs, ts, d // HEAD_DIM, HEAD_DIM))
            vs_l.append(v.reshape(bs, ts, d // V_DIM, V_DIM))
        else:
            s = i // 2
            nh = n_heads_ssm
            di = nh * SSM_HEAD_DIM
            cd = conv_w.shape[2]
            pad_h = LANES - nh
            win = bf(jnp.pad(w_in[s], ((0, 0), (0, pad_h))))
            dtb = jnp.pad(dt_bias[s], (0, pad_h)).reshape(1, LANES)
            alog = jnp.pad(a_log[s], (0, pad_h)).reshape(1, LANES)
            dsk = jnp.repeat(d_skip[s], SSM_HEAD_DIM).reshape(1, di)
            gain, gw, wout = row(ssm_norm[s]), row(gnorm_w[s]), bf(w_out[s])
            cw, cb = conv_w[s], row(conv_b[s])
            z, xbc, dtr = _inproj(hp, gain, win, di, cd, _tile(mp, 256))
            y, st = _ssd(xbc.reshape(bp, tp, cd), dtr.reshape(bp, tp, LANES), jnp.zeros((bp, SUBLANES, cd), F32),
                         jnp.zeros((bp, di, D_STATE), F32), cw, cb, dtb, alog, dsk, CHUNK)
            hp = _outproj(y.reshape(mp, di), z, hp, gw, wout, tm_p)
            sp_l.append(st.reshape(bp, nh, SSM_HEAD_DIM, D_STATE))
            cp_l.append(xbc.reshape(bp, tp, cd)[:, tp - (CONV_K - 1):])
            z, xbc, dtr = _inproj(hs, gain, win, di, cd, tm_s)
            xbc3 = xbc.reshape(bs, ts, cd)
            padt = ((0, 0), (0, CHUNK - ts), (0, 0))
            c0 = jnp.pad(state_conv[s], ((0, 0), (SUBLANES - (CONV_K - 1), 0), (0, 0)))
            y, st = _ssd(jnp.pad(xbc3, padt), jnp.pad(dtr.reshape(bs, ts, LANES), padt), c0,
                         state_ssm[s].reshape(bs, di, D_STATE), cw, cb, dtb, alog, dsk, ts)
            hs = _outproj(y[:, :ts].reshape(msz, di), z, hs, gw, wout, tm_s)
            ss_l.append(st.reshape(bs, nh, SSM_HEAD_DIM, D_STATE))
            cs_l.append(jnp.concatenate([state_conv[s], xbc3], axis=1)[:, ts:])
        gain, wu, wd = row(mlp_norm[i]), bf(w_up[i]), bf(w_down[i])
        hp = _mlp(hp, gain, wu, wd, tm_p)
        hs = _mlp(hs, gain, wu, wd, tm_s)
        gain, wg, wp = row(ple_norm[i]), bf(w_ple_gate[i]), bf(w_ple_proj[i])
        fin = row(final_norm) if i == depth - 1 else None
        rp = _ple(hp, p_prompt[i].reshape(mp, -1), gain, wg, wp, fin, tm_p)
        rs = _ple(hs, p_sample[i].reshape(msz, -1), gain, wg, wp, fin, tm_s)
        if fin is None:
            hp, hs = rp, rs
        else:
            (hp, yp), (hs, ys) = rp, rs

    return (yp.reshape(bp, tp, d), ys.reshape(bs, ts, d), jnp.stack(kp_l), jnp.stack(vp_l), jnp.stack(ks_l),
            jnp.stack(vs_l), jnp.stack(sp_l), jnp.stack(cp_l), jnp.stack(ss_l), jnp.stack(cs_l))
```

```python
import functools
import math

import jax
import jax.numpy as jnp
from jax import lax
from jax.experimental import pallas as pl
from jax.experimental.pallas import tpu as pltpu

F32 = jnp.float32
BF16 = jnp.bfloat16

HEAD_DIM = 64
V_DIM = 2 * HEAD_DIM
ROT_DIM = HEAD_DIM // 4
ROPE_THETA = 500000.0
SUBLN_EPS = 1e-5
EPS = 1e-6
NEG_INF = -1e30
LOG2E = 1.4426950408889634
SSM_HEAD_DIM = 64
SSM_GROUPS = 8
D_STATE = 128
CONV_K = 4
CHUNK = 128
LANES = 128
SUBLANES = 8
VMEM_LIMIT = 56 * 1024 * 1024


def _params(*sem):
    return pltpu.CompilerParams(dimension_semantics=sem, vmem_limit_bytes=VMEM_LIMIT)


def _resident(shape):
    return pl.BlockSpec(shape, lambda *_: (0,) * len(shape), pipeline_mode=pl.Buffered(1))


def _rms(x, g, eps):
    ms = jnp.mean(x * x, axis=-1, keepdims=True)
    return x * lax.rsqrt(ms + eps) * g


def _row_to_col(row):
    n = row.shape[1]
    eye = lax.broadcasted_iota(jnp.int32, (n, n), 0) == lax.broadcasted_iota(jnp.int32, (n, n), 1)
    return jnp.sum(jnp.where(eye, jnp.broadcast_to(row, (n, n)), 0.0), axis=1, keepdims=True)


def _diff_lambda(lq1, lk1, lq2, lk2, lam_init):
    s1 = jnp.sum(lq1[...] * lk1[...], axis=1, keepdims=True)
    s2 = jnp.sum(lq2[...] * lk2[...], axis=1, keepdims=True)
    return jnp.exp(s1) - jnp.exp(s2) + lam_init


def _qkv_kernel(x_ref, g_ref, w_ref, c_ref, s1_ref, s2_ref, q_ref, k_ref, v_ref, *, k_transposed):
    d = x_ref.shape[1]
    xn = _rms(x_ref[...], g_ref[...], EPS).astype(BF16)
    c, s1, s2 = c_ref[...], s1_ref[...], s2_ref[...]
    for part in range(2):
        y = jnp.dot(xn, w_ref[:, part * d:(part + 1) * d], preferred_element_type=F32)
        for hb in range(d // LANES):
            cols = slice(hb * LANES, (hb + 1) * LANES)
            blk = y[:, cols]
            rot = blk * c + pltpu.roll(blk, LANES - ROT_DIM // 2, 1) * s1 + pltpu.roll(blk, ROT_DIM // 2, 1) * s2
            if part == 0:
                q_ref[:, cols] = rot
            elif k_transposed:
                k_ref[0, cols, :] = rot.T
            else:
                k_ref[:, cols] = rot
    v_ref[...] = jnp.dot(xn, w_ref[:, 2 * d:3 * d], preferred_element_type=F32)


def _rope_tables(pos):
    half = ROT_DIM // 2
    inv_freq = jnp.power(ROPE_THETA, -jnp.arange(half, dtype=F32) * (2.0 / ROT_DIM))
    ang = pos.astype(F32)[:, None] * inv_freq[None, :]
    cos, sin = jnp.cos(ang), jnp.sin(ang)
    t = pos.shape[0]
    z8 = jnp.zeros((t, half), F32)
    rest0 = jnp.zeros((t, HEAD_DIM - ROT_DIM), F32)
    c = jnp.concatenate([cos, cos, jnp.ones((t, HEAD_DIM - ROT_DIM), F32)], axis=1)
    s1 = jnp.concatenate([-sin, z8, rest0], axis=1)
    s2 = jnp.concatenate([z8, sin, rest0], axis=1)
    rep = LANES // HEAD_DIM
    return tuple(jnp.tile(a, (1, rep)) for a in (c, s1, s2))


def _qkv(x, gain, w, tabs, tm, k_transposed):
    m, d = x.shape
    t = tabs[0].shape[0]
    nt = t // tm
    row = pl.BlockSpec((tm, d), lambda i: (i, 0))
    tab = pl.BlockSpec((tm, LANES), lambda i: (i % nt, 0))
    if k_transposed:
        kspec = pl.BlockSpec((1, d, tm), lambda i: (i // nt, 0, i % nt))
        kshape = jax.ShapeDtypeStruct((m // t, d, t), F32)
    else:
        kspec, kshape = row, jax.ShapeDtypeStruct((m, d), F32)
    return pl.pallas_call(
        functools.partial(_qkv_kernel, k_transposed=k_transposed),
        grid=(m // tm,),
        in_specs=[row, _resident((1, d)), _resident((d, 3 * d)), tab, tab, tab],
        out_specs=[row, kspec, row],
        out_shape=[jax.ShapeDtypeStruct((m, d), F32), kshape, jax.ShapeDtypeStruct((m, d), F32)],
        compiler_params=_params("parallel"),
        name="qkv",
    )(x, gain, w, *tabs)


def _attn_kernel(q_ref, k_ref, v_ref, lq1, lk1, lq2, lk2, sw_ref, o_ref,
                 kb, vtb, qbd_s, sa, sb, mxa, mxb, m_s, l_s, acc_s, *, tq, lam_init):
    qi = pl.program_id(2)
    nk = vtb.shape[0]

    @pl.when(qi == 0)
    def _():
        for c in range(nk):
            kb[c * tq:(c + 1) * tq, :] = k_ref[0, :, c * tq:(c + 1) * tq].T.astype(BF16)
            vtb[c] = v_ref[0, c * tq:(c + 1) * tq, :].T.astype(BF16)

    qt = (q_ref[0] * (HEAD_DIM ** -0.5 * LOG2E)).T
    rowi = lax.broadcasted_iota(jnp.int32, qt.shape, 0)
    qbd_s[...] = jnp.concatenate([jnp.where(rowi < HEAD_DIM, qt, 0.0), jnp.where(rowi >= HEAD_DIM, qt, 0.0)],
                                 axis=1).astype(BF16)
    m_s[...] = jnp.full(m_s.shape, NEG_INF, F32)
    l_s[...] = jnp.zeros(l_s.shape, F32)
    acc_s[...] = jnp.zeros(acc_s.shape, F32)

    def scores(blk, s_ref, mx_ref, masked):
        start = pl.multiple_of(blk * tq, tq)
        st = jnp.dot(kb[pl.ds(start, tq), :], qbd_s[...], preferred_element_type=F32)
        if masked:
            kj = lax.broadcasted_iota(jnp.int32, st.shape, 0)
            qc = lax.broadcasted_iota(jnp.int32, st.shape, 1)
            qc = jnp.where(qc >= tq, qc - tq, qc)
            st = jnp.where(kj <= qc, st, NEG_INF)
        s_ref[...] = st
        mx_ref[...] = jnp.max(st, axis=0, keepdims=True)

    def consume(blk, s_ref, mx_ref):
        m_prev = m_s[...]
        m_new = jnp.maximum(m_prev, mx_ref[...])
        alpha = jnp.exp2(m_prev - m_new)
        p = jnp.exp2(s_ref[...] - m_new)
        l_s[...] = alpha * l_s[...] + jnp.sum(p, axis=0, keepdims=True)
        acc_s[...] = alpha * acc_s[...] + jnp.dot(vtb[blk], p.astype(BF16), preferred_element_type=F32)
        m_s[...] = m_new

    scores(qi, sa, mxa, True)
    npair = qi // 2

    def body(jj, carry):
        scores(2 * jj, sb, mxb, False)
        consume(jnp.where(jj == 0, qi, 2 * jj - 1), sa, mxa)
        scores(2 * jj + 1, sa, mxa, False)
        consume(2 * jj, sb, mxb)
        return carry

    lax.fori_loop(0, npair, body, 0)
    blk_a = jnp.where(npair == 0, qi, 2 * npair - 1)

    @pl.when(qi % 2 == 0)
    def _():
        consume(blk_a, sa, mxa)

    @pl.when(qi % 2 == 1)
    def _():
        scores(qi - 1, sb, mxb, False)
        consume(blk_a, sa, mxa)
        consume(qi - 1, sb, mxb)

    a = acc_s[...] * (1.0 / l_s[...])
    lam = _diff_lambda(lq1, lk1, lq2, lk2, lam_init)
    ot = a[:, :tq] - lam * a[:, tq:]
    ms = jnp.mean(ot * ot, axis=0, keepdims=True)
    ot = ot * lax.rsqrt(ms + SUBLN_EPS) * sw_ref[...] * (1.0 - lam_init)
    o_ref[0] = ot.T.astype(o_ref.dtype)


def _prompt_attention(q, k, v, lam_params, subln_col, lam_init, tq):
    b, t, d = q.shape
    hp = d // V_DIM
    qspec = pl.BlockSpec((1, tq, V_DIM), lambda bi, h, qi: (bi, qi, h))
    kspec = pl.BlockSpec((1, V_DIM, t), lambda bi, h, qi: (bi, h, 0))
    vspec = pl.BlockSpec((1, t, V_DIM), lambda bi, h, qi: (bi, 0, h))
    lspec = _resident((1, HEAD_DIM))
    return pl.pallas_call(
        functools.partial(_attn_kernel, tq=tq, lam_init=lam_init),
        grid=(b, hp, t // tq),
        in_specs=[qspec, kspec, vspec, lspec, lspec, lspec, lspec, _resident((V_DIM, 1))],
        out_specs=qspec,
        out_shape=jax.ShapeDtypeStruct((b, t, d), BF16),
        scratch_shapes=[pltpu.VMEM((t, V_DIM), BF16), pltpu.VMEM((t // tq, V_DIM, tq), BF16),
                        pltpu.VMEM((V_DIM, 2 * tq), BF16),
                        pltpu.VMEM((tq, 2 * tq), F32), pltpu.VMEM((tq, 2 * tq), F32),
                        pltpu.VMEM((1, 2 * tq), F32), pltpu.VMEM((1, 2 * tq), F32),
                        pltpu.VMEM((1, 2 * tq), F32), pltpu.VMEM((1, 2 * tq), F32),
                        pltpu.VMEM((V_DIM, 2 * tq), F32)],
        compiler_params=_params("parallel", "parallel", "arbitrary"),
        name="prompt_attn",
    )(q, k, v, *lam_params, subln_col)


def _sattn_kernel(pt_ref, q_ref, kn_ref, vn_ref, *rest, pps, lam_init, n_new):
    del pt_ref
    k_refs, v_refs = rest[:pps], rest[pps:2 * pps]
    lq1, lk1, lq2, lk2, sw_ref, o_ref, m_s, l_s, acc_s = rest[2 * pps:]
    g = pl.program_id(1)
    ng = pl.num_programs(1)
    d = q_ref.shape[2]
    nc = d // HEAD_DIM
    nh = d // V_DIM
    page = kn_ref.shape[3]
    rows = nc * SUBLANES

    @pl.when(g == 0)
    def _():
        m_s[...] = jnp.full(m_s.shape, NEG_INF, F32)
        l_s[...] = jnp.zeros(l_s.shape, F32)
        acc_s[...] = jnp.zeros(acc_s.shape, F32)

    q = (q_ref[0] * (HEAD_DIM ** -0.5)).astype(BF16)
    qcs = [q[:, c * HEAD_DIM:(c + 1) * HEAD_DIM] for c in range(nc)]

    def update(kv_pairs, mask):
        scores = []
        for kr, _ in kv_pairs:
            s = jnp.concatenate([jnp.dot(qcs[c], kr[c].astype(BF16), preferred_element_type=F32) for c in range(nc)],
                                axis=0)
            if mask is not None:
                s = jnp.where(mask, s, NEG_INF)
            scores.append(s)
        mx = scores[0]
        for s in scores[1:]:
            mx = jnp.maximum(mx, s)
        m_prev = m_s[...]
        m_new = jnp.maximum(m_prev, jnp.max(mx, axis=1, keepdims=True))
        alpha = jnp.exp(m_prev - m_new)
        psum = None
        pv = [None] * nh
        for s, (_, vr) in zip(scores, kv_pairs):
            p = jnp.exp(s - m_new)
            psum = p if psum is None else psum + p
            pb = p.astype(BF16)
            for h in range(nh):
                vh = vr[pl.ds(h, page, stride=nh), :].astype(BF16)
                t = jnp.dot(pb[2 * h * SUBLANES:(2 * h + 2) * SUBLANES, :], vh, preferred_element_type=F32)
                pv[h] = t if pv[h] is None else pv[h] + t
        acc_s[...] = alpha * acc_s[...] + jnp.concatenate(pv, axis=0)
        l_s[...] = alpha * l_s[...] + jnp.sum(psum, axis=1, keepdims=True)
        m_s[...] = m_new

    update(list(zip(k_refs, v_refs)), None)

    @pl.when(g == ng - 1)
    def _():
        kj = lax.broadcasted_iota(jnp.int32, (rows, page), 1)
        slot = lax.broadcasted_iota(jnp.int32, (rows, page), 0) % SUBLANES
        mask = (kj < n_new) & (kj <= slot)
        update([(kn_ref.at[0], vn_ref.at[0])], mask)
        a = acc_s[...] * (1.0 / l_s[...])
        lam = _diff_lambda(lq1, lk1, lq2, lk2, lam_init)
        for h in range(nh):
            a1 = a[(2 * h) * SUBLANES:(2 * h + 1) * SUBLANES, :]
            a2 = a[(2 * h + 1) * SUBLANES:(2 * h + 2) * SUBLANES, :]
            o = a1 - lam * a2
            o_ref[0, :, h * V_DIM:(h + 1) * V_DIM] = _rms(o, sw_ref[...], SUBLN_EPS) * (1.0 - lam_init)


def _sample_attention(q, k_new, v_new, cache_k, cache_v, layer, page_table, lam_params, subln_row, lam_init, pps):
    s, n_new, d = q.shape
    na, n_pool, page, nc, _ = cache_k.shape
    nh = cache_v.shape[3]
    n_pages = page_table.shape[1]
    rows = nc * SUBLANES
    qp = jnp.pad(q, ((0, 0), (0, SUBLANES - n_new), (0, 0)))
    ck = jnp.transpose(cache_k, (0, 1, 3, 4, 2))
    cv = cache_v.reshape(na, n_pool, page * nh, V_DIM)
    knp = jnp.pad(jnp.transpose(k_new.reshape(s, n_new, nc, HEAD_DIM), (0, 2, 3, 1)),
                  ((0, 0), (0, 0), (0, 0), (0, page - n_new)))
    vnp = jnp.pad(v_new, ((0, 0), (0, page - n_new), (0, 0))).reshape(s, page * nh, V_DIM)

    kspec = lambda j: pl.BlockSpec((None, None, nc, HEAD_DIM, page),
                                   lambda si, g, pt: (layer, pt[si, g * pps + j], 0, 0, 0))
    vspec = lambda j: pl.BlockSpec((None, None, page * nh, V_DIM), lambda si, g, pt: (layer, pt[si, g * pps + j], 0, 0))
    seq8 = pl.BlockSpec((1, SUBLANES, d), lambda si, g, pt: (si, 0, 0))
    knspec = pl.BlockSpec((1, nc, HEAD_DIM, page), lambda si, g, pt: (si, 0, 0, 0))
    vnspec = pl.BlockSpec((1, page * nh, V_DIM), lambda si, g, pt: (si, 0, 0))
    small = lambda shape: pl.BlockSpec(shape, lambda si, g, pt: (0,) * len(shape))

    out = pl.pallas_call(
        functools.partial(_sattn_kernel, pps=pps, lam_init=lam_init, n_new=n_new),
        grid_spec=pltpu.PrefetchScalarGridSpec(
            num_scalar_prefetch=1,
            grid=(s, n_pages // pps),
            in_specs=[seq8, knspec, vnspec] + [kspec(j) for j in range(pps)] + [vspec(j) for j in range(pps)]
            + [small((1, HEAD_DIM))] * 4 + [small((1, V_DIM))],
            out_specs=seq8,
            scratch_shapes=[pltpu.VMEM((rows, 1), F32), pltpu.VMEM((rows, 1), F32), pltpu.VMEM((rows, V_DIM), F32)],
        ),
        out_shape=jax.ShapeDtypeStruct((s, SUBLANES, d), F32),
        compiler_params=_params("parallel", "arbitrary"),
        name="sample_attn",
    )(page_table, qp, knp, vnp, *([ck] * pps), *([cv] * pps), *lam_params, subln_row)
    return out[:, :n_new]


def _post_kernel(*refs, ssm_front, final, fc):
    if ssm_front:
        y_ref, z_ref, h_ref, gw_ref, wo_ref = refs[:5]
        rest = refs[5:]
        z = z_ref[...]
        yg = y_ref[...] * (z * jax.nn.sigmoid(z))
        gs = yg.shape[1] // SSM_GROUPS
        mix = jnp.concatenate([_rms(yg[:, g * gs:(g + 1) * gs], gw_ref[:, g * gs:(g + 1) * gs], EPS).astype(BF16)
                               for g in range(SSM_GROUPS)], axis=1)
    else:
        o_ref, h_ref, wo_ref = refs[:3]
        rest = refs[3:]
        mix = o_ref[...].astype(BF16)
    gm_ref, wu_ref, wd_ref, p_ref, gp_ref, wg_ref, wp_ref = rest[:7]
    h = h_ref[...] + jnp.dot(mix, wo_ref[...], preferred_element_type=F32)
    xn = _rms(h, gm_ref[...], EPS).astype(BF16)
    for c in range(wu_ref.shape[1] // fc):
        a = jnp.maximum(jnp.dot(xn, wu_ref[:, c * fc:(c + 1) * fc], preferred_element_type=F32), 0.0)
        h = h + jnp.dot((a * a).astype(BF16), wd_ref[c * fc:(c + 1) * fc, :], preferred_element_type=F32)
    gate = jax.nn.sigmoid(jnp.dot(_rms(h, gp_ref[...], EPS).astype(BF16), wg_ref[...], preferred_element_type=F32))
    out = h + gate * jnp.dot(p_ref[...].astype(BF16), wp_ref[...], preferred_element_type=F32)
    if final:
        fn_ref, out_ref = rest[7:]
        out_ref[...] = _rms(out, fn_ref[...], EPS)
    else:
        (out_ref,) = rest[7:]
        out_ref[...] = out


def _post(mix_in, h, mix_w, mlp_w, ple_in, final_gain, tm):
    m, d = h.shape
    ssm_front = len(mix_in) == 3
    gm, wu, wd = mlp_w
    p, gp, wg, wp = ple_in
    ff, pd = wu.shape[1], p.shape[1]
    rows = lambda width: pl.BlockSpec((tm, width), lambda i: (i, 0))
    if ssm_front:
        y, z, gw = mix_in
        di = y.shape[1]
        ins = [y, z, h, gw, mix_w]
        specs = [rows(di), rows(di), rows(d), _resident((1, di)), _resident((di, d))]
    else:
        ins = [mix_in[0], h, mix_w]
        specs = [rows(d), rows(d), _resident((d, d))]
    ins += [gm, wu, wd, p, gp, wg, wp]
    specs += [_resident((1, d)), _resident((d, ff)), _resident((ff, d)), rows(pd), _resident((1, d)),
              _resident((d, d)), _resident((pd, d))]
    final = final_gain is not None
    if final:
        ins.append(final_gain)
        specs.append(_resident((1, d)))
    return pl.pallas_call(
        functools.partial(_post_kernel, ssm_front=ssm_front, final=final, fc=min(ff, 1024)), grid=(m // tm,),
        in_specs=specs, out_specs=rows(d), out_shape=jax.ShapeDtypeStruct((m, d), F32),
        compiler_params=_params("parallel"), name="post_ssm" if ssm_front else "post_attn",
    )(*ins)


def _inproj_kernel(x_ref, g_ref, w_ref, z_ref, xbc_ref, dt_ref):
    xn = _rms(x_ref[...], g_ref[...], EPS).astype(BF16)
    nz, nx = z_ref.shape[1], xbc_ref.shape[1]
    z_ref[...] = jnp.dot(xn, w_ref[:, :nz], preferred_element_type=F32)
    xbc_ref[...] = jnp.dot(xn, w_ref[:, nz:nz + nx], preferred_element_type=F32)
    dt_ref[...] = jnp.dot(xn, w_ref[:, nz + nx:], preferred_element_type=F32)


def _inproj(x, gain, w, d_inner, conv_dim, tm):
    m, d = x.shape
    n = w.shape[1]
    ndt = n - d_inner - conv_dim
    row = lambda width: pl.BlockSpec((tm, width), lambda i: (i, 0))
    return pl.pallas_call(
        _inproj_kernel, grid=(m // tm,),
        in_specs=[row(d), _resident((1, d)), _resident((d, n))],
        out_specs=[row(d_inner), row(conv_dim), row(ndt)],
        out_shape=[jax.ShapeDtypeStruct((m, d_inner), F32), jax.ShapeDtypeStruct((m, conv_dim), F32),
                   jax.ShapeDtypeStruct((m, ndt), F32)],
        compiler_params=_params("parallel"), name="inproj",
    )(x, gain, w)


def _inproj_conv_kernel(x_ref, g_ref, w_ref, cw_ref, cb_ref, z_ref, xs_ref, bc_ref, dt_ref, tail_ref, hist_s,
                        *, tiles_per_seq, cblk):
    i = pl.program_id(0)
    tm = x_ref.shape[0]
    nz, nx, nbc = z_ref.shape[1], xs_ref.shape[1], bc_ref.shape[1]
    cd = nx + nbc
    xn = _rms(x_ref[...], g_ref[...], EPS).astype(BF16)

    @pl.when(i % tiles_per_seq == 0)
    def _():
        hist_s[...] = jnp.zeros(hist_s.shape, F32)

    nblk = cd // cblk
    zb = [(nz // LANES * k // nblk) * LANES for k in range(nblk + 1)]
    for k in range(nblk):
        c0 = k * cblk
        cols = slice(c0, c0 + cblk)
        raw = jnp.dot(xn, w_ref[:, nz + c0:nz + c0 + cblk], preferred_element_type=F32)
        if zb[k + 1] > zb[k]:
            z_ref[:, zb[k]:zb[k + 1]] = jnp.dot(xn, w_ref[:, zb[k]:zb[k + 1]], preferred_element_type=F32)
        act = _conv_silu(jnp.concatenate([hist_s[:, cols], raw], axis=0), cw_ref, cb_ref, cols)
        hist_s[:, cols] = raw[tm - SUBLANES:, :]
        if c0 < nx:
            xs_ref[:, cols] = act
        else:
            bc_ref[:, c0 - nx:c0 - nx + cblk] = act
    dt_ref[...] = jnp.dot(xn, w_ref[:, nz + cd:], preferred_element_type=F32)
    tail_ref[0] = hist_s[...]


def _inproj_conv(x, gain, w, cw, cb, d_inner, t, tm):
    m, d = x.shape
    n, cd = w.shape[1], cw.shape[1]
    nbc, ndt = cd - d_inner, n - d_inner - cd
    cblk = 512
    assert d_inner % cblk == 0 and nbc % cblk == 0 and t % tm == 0
    tps = t // tm
    row = lambda width: pl.BlockSpec((tm, width), lambda i: (i, 0))
    sds = lambda width: jax.ShapeDtypeStruct((m, width), F32)
    return pl.pallas_call(
        functools.partial(_inproj_conv_kernel, tiles_per_seq=tps, cblk=cblk), grid=(m // tm,),
        in_specs=[row(d), _resident((1, d)), _resident((d, n)), _resident((CONV_K, cd)), _resident((1, cd))],
        out_specs=[row(d_inner), row(d_inner), row(nbc), row(ndt),
                   pl.BlockSpec((1, SUBLANES, cd), lambda i: (i // tps, 0, 0))],
        out_shape=[sds(d_inner), sds(d_inner), sds(nbc), sds(ndt), jax.ShapeDtypeStruct((m // t, SUBLANES, cd), F32)],
        scratch_shapes=[pltpu.VMEM((SUBLANES, cd), F32)],
        compiler_params=_params("arbitrary"), name="inproj_conv",
    )(x, gain, w, cw, cb)


def _softplus(x):
    return jnp.maximum(x, 0.0) + jnp.log1p(jnp.exp(-jnp.abs(x)))


def _conv_silu(w, cw_ref, cb_ref, cols):
    conv = cb_ref[:, cols] + cw_ref[CONV_K - 1:CONV_K, cols] * w[SUBLANES:]
    for s in range(1, CONV_K):
        tap = CONV_K - 1 - s
        conv = conv + cw_ref[tap:tap + 1, cols] * pltpu.roll(w, s, 0)[SUBLANES:]
    return conv * jax.nn.sigmoid(conv)


def _ssd_conv_kernel(xbc_ref, dt_ref, c0_ref, h0_ref, cw_ref, cb_ref, dtb_ref, alog_ref, dsk_ref,
                     y_ref, hout_ref, ht_s, win_s, act_s, *, valid, n_heads):
    L = xbc_ref.shape[1]
    di = n_heads * SSM_HEAD_DIM

    @pl.when(pl.program_id(1) == 0)
    def _():
        win_s[0:SUBLANES, :] = c0_ref[0]

    win_s[SUBLANES:SUBLANES + L, :] = xbc_ref[0]
    act_s[...] = _conv_silu(win_s[0:L + SUBLANES, :], cw_ref, cb_ref, slice(None))
    win_s[0:SUBLANES, :] = win_s[L:L + SUBLANES, :]
    _ssd_chunk(lambda cols: act_s[:, cols], lambda s: act_s[:, di + s:di + s + D_STATE],
               dt_ref, h0_ref, dtb_ref, alog_ref, dsk_ref, y_ref, hout_ref, ht_s, valid=valid, n_heads=n_heads)


def _ssd_act_kernel(xs_ref, bc_ref, dt_ref, h0_ref, dtb_ref, alog_ref, dsk_ref, y_ref, hout_ref, ht_s,
                    *, valid, n_heads):
    _ssd_chunk(lambda cols: xs_ref[0, :, cols], lambda s: bc_ref[0, :, s:s + D_STATE],
               dt_ref, h0_ref, dtb_ref, alog_ref, dsk_ref, y_ref, hout_ref, ht_s, valid=valid, n_heads=n_heads)


def _ssd_chunk(x_of, bc_of, dt_ref, h0_ref, dtb_ref, alog_ref, dsk_ref, y_ref, hout_ref, ht_s, *, valid, n_heads):
    c = pl.program_id(1)
    nchunks = pl.num_programs(1)
    L = dt_ref.shape[1]
    hpg = n_heads // SSM_GROUPS
    gw = hpg * SSM_HEAD_DIM

    @pl.when(c == 0)
    def _():
        ht_s[...] = h0_ref[0].T

    dt = _softplus(dt_ref[0] + dtb_ref[...])
    lane = lax.broadcasted_iota(jnp.int32, dt.shape, 1)
    ok = lane < n_heads
    if valid < L:
        ok = ok & (lax.broadcasted_iota(jnp.int32, dt.shape, 0) < valid)
    dt = jnp.where(ok, dt, 0.0)
    dta = dt * (-jnp.exp(alog_ref[...]))
    tril = lax.broadcasted_iota(jnp.int32, (L, L), 0) >= lax.broadcasted_iota(jnp.int32, (L, L), 1)
    trilb = jnp.where(tril, 1.0, 0.0).astype(BF16)
    hi = dta.astype(BF16)
    r1 = dta - hi.astype(F32)
    mid = r1.astype(BF16)
    lo = (r1 - mid.astype(F32)).astype(BF16)
    acs = (jnp.dot(trilb, hi, preferred_element_type=F32) + jnp.dot(trilb, mid, preferred_element_type=F32)
           + jnp.dot(trilb, lo, preferred_element_type=F32))
    acs_t = acs.T
    last = acs[L - 1:L, :]
    pw = 2 * SSM_HEAD_DIM
    first = lax.broadcasted_iota(jnp.int32, (L, pw), 1) < SSM_HEAD_DIM

    def pair(a, h0, rows):
        return jnp.where(first[:rows], jnp.broadcast_to(a[:, h0:h0 + 1], (rows, pw)),
                         jnp.broadcast_to(a[:, h0 + 1:h0 + 2], (rows, pw)))

    for g in range(SSM_GROUPS):
        bg_t = bc_of(g * D_STATE).T.astype(BF16)
        cg = bc_of((SSM_GROUPS + g) * D_STATE).astype(BF16)
        cbm = jnp.dot(cg, bg_t, preferred_element_type=F32)
        hg = ht_s[:, g * gw:(g + 1) * gw]
        yin = jnp.dot(cg, hg.astype(BF16), preferred_element_type=F32)
        xw, el = [], []
        for pr in range(hpg // 2):
            h0 = g * hpg + 2 * pr
            cols = slice(h0 * SSM_HEAD_DIM, (h0 + 2) * SSM_HEAD_DIM)
            x2 = x_of(cols)
            acol = pair(acs, h0, L)
            last2 = pair(last, h0, 1)
            xdt = x2 * pair(dt, h0, L)
            xdtb = xdt.astype(BF16)
            ys = []
            for r in range(2):
                col = jnp.broadcast_to(acs[:, h0 + r:h0 + r + 1], (L, L))
                dec = jnp.exp(jnp.where(tril, col - acs_t[h0 + r:h0 + r + 1, :], NEG_INF))
                ys.append(jnp.dot((cbm * dec).astype(BF16), xdtb, preferred_element_type=F32))
            y_ref[0, :, cols] = (jnp.where(first, ys[0], ys[1]) + yin[:, pr * pw:(pr + 1) * pw] * jnp.exp(acol)
                                 + dsk_ref[:, cols] * x2)
            xw.append((xdt * jnp.exp(last2 - acol)).astype(BF16))
            el.append(jnp.exp(last2))
        xw = jnp.concatenate(xw, axis=1) if len(xw) > 1 else xw[0]
        el = jnp.concatenate(el, axis=1) if len(el) > 1 else el[0]
        ht_s[:, g * gw:(g + 1) * gw] = hg * el + jnp.dot(bg_t, xw, preferred_element_type=F32)

    @pl.when(c == nchunks - 1)
    def _():
        hout_ref[0] = ht_s[...].T


def _ssd(x_in, dt_raw, h0, dtb, alog, dsk, valid, conv=None):
    b, t, _ = dt_raw.shape
    di = h0.shape[1]
    L = min(CHUNK, t)
    chunk = lambda width: pl.BlockSpec((1, L, width), lambda bi, c: (bi, c, 0))
    per_b = lambda rows, width: pl.BlockSpec((1, rows, width), lambda bi, c: (bi, 0, 0))
    small = lambda shape: pl.BlockSpec(shape, lambda bi, c: (0,) * len(shape))
    tail_specs = [small((1, LANES)), small((1, LANES)), small((1, di))]
    scratch = [pltpu.VMEM((D_STATE, di), F32)]
    if conv is not None:
        (xbc,), (c0, cw, cb) = x_in, conv
        cd = xbc.shape[2]
        body = _ssd_conv_kernel
        ins = [xbc, dt_raw, c0, h0, cw, cb, dtb, alog, dsk]
        specs = [chunk(cd), chunk(LANES), per_b(SUBLANES, cd), per_b(di, D_STATE), small((CONV_K, cd)),
                 small((1, cd))] + tail_specs
        scratch += [pltpu.VMEM((L + 2 * SUBLANES, cd), F32), pltpu.VMEM((L, cd), F32)]
    else:
        xs, bc = x_in
        body = _ssd_act_kernel
        ins = [xs, bc, dt_raw, h0, dtb, alog, dsk]
        specs = [chunk(di), chunk(bc.shape[2]), chunk(LANES), per_b(di, D_STATE)] + tail_specs
    return pl.pallas_call(
        functools.partial(body, valid=valid, n_heads=di // SSM_HEAD_DIM), grid=(b, t // L),
        in_specs=specs, out_specs=[chunk(di), per_b(di, D_STATE)],
        out_shape=[jax.ShapeDtypeStruct((b, t, di), F32), jax.ShapeDtypeStruct((b, di, D_STATE), F32)],
        scratch_shapes=scratch,
        compiler_params=_params("parallel", "arbitrary"), name="ssd",
    )(*ins)


def _tile(m, want):
    return want if m % want == 0 else m


def kernel(x_prompt, x_sample, cache_k, cache_v, state_ssm, state_conv, page_table, p_prompt, p_sample, attn_norm, w_qkv, lambda_q1, lambda_k1, lambda_q2, lambda_k2, subln_w, w_o, ssm_norm, w_in, conv_w, conv_b, dt_bias, a_log, d_skip, gnorm_w, w_out, mlp_norm, w_up, w_down, ple_norm, w_ple_gate, w_ple_proj, final_norm):
    bp, tp, d = x_prompt.shape
    bs, ts, _ = x_sample.shape
    depth = mlp_norm.shape[0]
    past_len = page_table.shape[1] * cache_k.shape[2]
    n_heads_ssm = dt_bias.shape[1] if dt_bias.ndim == 2 else 0
    mp, msz = bp * tp, bs * ts
    tm_p = _tile(mp, 512)
    tm_s = msz
    bf = lambda a: a.astype(BF16)
    row = lambda a: a.reshape(1, -1)

    hp = x_prompt.reshape(mp, d)
    hs = x_sample.reshape(msz, d)
    tabs_p = _rope_tables(jnp.arange(tp, dtype=jnp.int32))
    tabs_s = _rope_tables(jnp.tile(past_len + jnp.arange(ts, dtype=jnp.int32), bs))

    kp_l, vp_l, ks_l, vs_l, sp_l, cp_l, ss_l, cs_l = [], [], [], [], [], [], [], []
    for i in range(depth):
        mlp_w = (row(mlp_norm[i]), bf(w_up[i]), bf(w_down[i]))
        ple_w = (row(ple_norm[i]), bf(w_ple_gate[i]), bf(w_ple_proj[i]))
        ple_p = (p_prompt[i].reshape(mp, -1),) + ple_w
        ple_s = (p_sample[i].reshape(msz, -1),) + ple_w
        fin = row(final_norm) if i == depth - 1 else None
        if i % 2 == 0:
            a = i // 2
            lam_init = 0.8 - 0.6 * math.exp(-0.3 * i)
            lam_params = [row(lambda_q1[a]), row(lambda_k1[a]), row(lambda_q2[a]), row(lambda_k2[a])]
            wqkv, wo, gain = bf(w_qkv[a]), bf(w_o[a]), row(attn_norm[a])
            q, kt, v = _qkv(hp, gain, wqkv, tabs_p, _tile(tp, 512), True)
            o = _prompt_attention(q.reshape(bp, tp, d), kt, v.reshape(bp, tp, d), lam_params,
                                  subln_w[a].reshape(V_DIM, 1), lam_init, _tile(tp, 512))
            hp = _post((o.reshape(mp, d),), hp, wo, mlp_w, ple_p, fin, tm_p)
            kp_l.append(jnp.transpose(kt.reshape(bp, d // HEAD_DIM, HEAD_DIM, tp), (0, 3, 1, 2)))
            vp_l.append(v.reshape(bp, tp, d // V_DIM, V_DIM))
            q, k, v = _qkv(hs, gain, wqkv, tabs_s, tm_s, False)
            o = _sample_attention(q.reshape(bs, ts, d), k.reshape(bs, ts, d), v.reshape(bs, ts, d), cache_k, cache_v,
                                  a, page_table, lam_params, row(subln_w[a]), lam_init, 8)
            hs = _post((o.reshape(msz, d),), hs, wo, mlp_w, ple_s, fin, tm_s)
            ks_l.append(k.reshape(bs, ts, d // HEAD_DIM, HEAD_DIM))
            vs_l.append(v.reshape(bs, ts, d // V_DIM, V_DIM))
        else:
            s = i // 2
            nh = n_heads_ssm
            di = nh * SSM_HEAD_DIM
            cd = conv_w.shape[2]
            pad_h = LANES - nh
            win = bf(jnp.pad(w_in[s], ((0, 0), (0, pad_h))))
            dtb = jnp.pad(dt_bias[s], (0, pad_h)).reshape(1, LANES)
            alog = jnp.pad(a_log[s], (0, pad_h)).reshape(1, LANES)
            dsk = jnp.repeat(d_skip[s], SSM_HEAD_DIM).reshape(1, di)
            gain, gw, wout = row(ssm_norm[s]), row(gnorm_w[s]), bf(w_out[s])
            cw, cb = conv_w[s], row(conv_b[s])
            z, xs, bca, dtr, tail = _inproj_conv(hp, gain, win, cw, cb, di, tp, _tile(tp, 256))
            y, st = _ssd((xs.reshape(bp, tp, di), bca.reshape(bp, tp, cd - di)), dtr.reshape(bp, tp, LANES),
                         jnp.zeros((bp, di, D_STATE), F32), dtb, alog, dsk, CHUNK)
            hp = _post((y.reshape(mp, di), z, gw), hp, wout, mlp_w, ple_p, fin, _tile(mp, 256))
            sp_l.append(st.reshape(bp, nh, SSM_HEAD_DIM, D_STATE))
            cp_l.append(tail[:, SUBLANES - (CONV_K - 1):])
            z, xbc, dtr = _inproj(hs, gain, win, di, cd, tm_s)
            xbc3 = xbc.reshape(bs, ts, cd)
            padt = ((0, 0), (0, CHUNK - ts), (0, 0))
            c0 = jnp.pad(state_conv[s], ((0, 0), (SUBLANES - (CONV_K - 1), 0), (0, 0)))
            y, st = _ssd((jnp.pad(xbc3, padt),), jnp.pad(dtr.reshape(bs, ts, LANES), padt),
                         state_ssm[s].reshape(bs, di, D_STATE), dtb, alog, dsk, ts, conv=(c0, cw, cb))
            hs = _post((y[:, :ts].reshape(msz, di), z, gw), hs, wout, mlp_w, ple_s, fin, tm_s)
            ss_l.append(st.reshape(bs, nh, SSM_HEAD_DIM, D_STATE))
            cs_l.append(jnp.concatenate([state_conv[s], xbc3], axis=1)[:, ts:])

    return (hp.reshape(bp, tp, d), hs.reshape(bs, ts, d), jnp.stack(kp_l), jnp.stack(vp_l), jnp.stack(ks_l),
            jnp.stack(vs_l), jnp.stack(sp_l), jnp.stack(cp_l), jnp.stack(ss_l), jnp.stack(cs_l))
```

```python
import functools
import math

import jax
import jax.numpy as jnp
from jax import lax
from jax.experimental import pallas as pl
from jax.experimental.pallas import tpu as pltpu

F32 = jnp.float32
BF16 = jnp.bfloat16

HEAD_DIM = 64
V_DIM = 2 * HEAD_DIM
ROT_DIM = HEAD_DIM // 4
ROPE_THETA = 500000.0
SUBLN_EPS = 1e-5
EPS = 1e-6
NEG_INF = -1e30
LOG2E = 1.4426950408889634
SSM_HEAD_DIM = 64
SSM_GROUPS = 8
D_STATE = 128
CONV_K = 4
CHUNK = 128
LANES = 128
SUBLANES = 8
VMEM_LIMIT = 56 * 1024 * 1024


def _params(*sem):
    return pltpu.CompilerParams(dimension_semantics=sem, vmem_limit_bytes=VMEM_LIMIT)


def _resident(shape):
    return pl.BlockSpec(shape, lambda *_: (0,) * len(shape), pipeline_mode=pl.Buffered(1))


def _rms(x, g, eps):
    ms = jnp.mean(x * x, axis=-1, keepdims=True)
    return x * lax.rsqrt(ms + eps) * g


def _row_to_col(row):
    n = row.shape[1]
    eye = lax.broadcasted_iota(jnp.int32, (n, n), 0) == lax.broadcasted_iota(jnp.int32, (n, n), 1)
    return jnp.sum(jnp.where(eye, jnp.broadcast_to(row, (n, n)), 0.0), axis=1, keepdims=True)


def _diff_lambda(lq1, lk1, lq2, lk2, lam_init):
    s1 = jnp.sum(lq1[...] * lk1[...], axis=1, keepdims=True)
    s2 = jnp.sum(lq2[...] * lk2[...], axis=1, keepdims=True)
    return jnp.exp(s1) - jnp.exp(s2) + lam_init


def _qkv_kernel(x_ref, g_ref, w_ref, c_ref, s1_ref, s2_ref, q_ref, k_ref, v_ref, *, k_transposed):
    d = x_ref.shape[1]
    xn = _rms(x_ref[...], g_ref[...], EPS).astype(BF16)
    c, s1, s2 = c_ref[...], s1_ref[...], s2_ref[...]
    for part in range(2):
        y = jnp.dot(xn, w_ref[:, part * d:(part + 1) * d], preferred_element_type=F32)
        for hb in range(d // LANES):
            cols = slice(hb * LANES, (hb + 1) * LANES)
            blk = y[:, cols]
            rot = blk * c + pltpu.roll(blk, LANES - ROT_DIM // 2, 1) * s1 + pltpu.roll(blk, ROT_DIM // 2, 1) * s2
            if part == 0:
                q_ref[:, cols] = rot
            elif k_transposed:
                k_ref[0, cols, :] = rot.T
            else:
                k_ref[:, cols] = rot
    v_ref[...] = jnp.dot(xn, w_ref[:, 2 * d:3 * d], preferred_element_type=F32)


def _rope_tables(pos):
    half = ROT_DIM // 2
    inv_freq = jnp.power(ROPE_THETA, -jnp.arange(half, dtype=F32) * (2.0 / ROT_DIM))
    ang = pos.astype(F32)[:, None] * inv_freq[None, :]
    cos, sin = jnp.cos(ang), jnp.sin(ang)
    t = pos.shape[0]
    z8 = jnp.zeros((t, half), F32)
    rest0 = jnp.zeros((t, HEAD_DIM - ROT_DIM), F32)
    c = jnp.concatenate([cos, cos, jnp.ones((t, HEAD_DIM - ROT_DIM), F32)], axis=1)
    s1 = jnp.concatenate([-sin, z8, rest0], axis=1)
    s2 = jnp.concatenate([z8, sin, rest0], axis=1)
    rep = LANES // HEAD_DIM
    return tuple(jnp.tile(a, (1, rep)) for a in (c, s1, s2))


def _qkv(x, gain, w, tabs, tm, k_transposed):
    m, d = x.shape
    t = tabs[0].shape[0]
    nt = t // tm
    row = pl.BlockSpec((tm, d), lambda i: (i, 0))
    tab = pl.BlockSpec((tm, LANES), lambda i: (i % nt, 0))
    if k_transposed:
        kspec = pl.BlockSpec((1, d, tm), lambda i: (i // nt, 0, i % nt))
        kshape = jax.ShapeDtypeStruct((m // t, d, t), F32)
    else:
        kspec, kshape = row, jax.ShapeDtypeStruct((m, d), F32)
    return pl.pallas_call(
        functools.partial(_qkv_kernel, k_transposed=k_transposed),
        grid=(m // tm,),
        in_specs=[row, _resident((1, d)), _resident((d, 3 * d)), tab, tab, tab],
        out_specs=[row, kspec, row],
        out_shape=[jax.ShapeDtypeStruct((m, d), F32), kshape, jax.ShapeDtypeStruct((m, d), F32)],
        compiler_params=_params("parallel"),
        name="qkv",
    )(x, gain, w, *tabs)


def _attn_kernel(qlo_ref, qhi_ref, k_ref, v_ref, lq1, lk1, lq2, lk2, sw_ref, olo_ref, ohi_ref,
                 kb, vtb, qbd_s, sa, sb, mxa, mxb, m_s, l_s, acc_s, *, tq, lam_init):
    p = pl.program_id(2)
    nt = vtb.shape[0]
    tiles = (p, nt - 1 - p)

    @pl.when(p == 0)
    def _():
        for c in range(nt):
            kb[c * tq:(c + 1) * tq, :] = k_ref[0, :, c * tq:(c + 1) * tq].T.astype(BF16)
            vtb[c] = v_ref[0, c * tq:(c + 1) * tq, :].T.astype(BF16)

    for sel, q_ref in enumerate((qlo_ref, qhi_ref)):
        qt = (q_ref[0] * (HEAD_DIM ** -0.5 * LOG2E)).T
        rowi = lax.broadcasted_iota(jnp.int32, qt.shape, 0)
        qbd_s[sel] = jnp.concatenate([jnp.where(rowi < HEAD_DIM, qt, 0.0), jnp.where(rowi >= HEAD_DIM, qt, 0.0)],
                                     axis=1).astype(BF16)
    m_s[...] = jnp.full(m_s.shape, NEG_INF, F32)
    l_s[...] = jnp.zeros(l_s.shape, F32)
    acc_s[...] = jnp.zeros(acc_s.shape, F32)

    def scores(sel, blk, masked, s_ref, mx_ref):
        start = pl.multiple_of(blk * tq, tq)
        st = jnp.dot(kb[pl.ds(start, tq), :], qbd_s[sel], preferred_element_type=F32)
        if masked:
            kj = lax.broadcasted_iota(jnp.int32, st.shape, 0)
            qc = lax.broadcasted_iota(jnp.int32, st.shape, 1)
            qc = jnp.where(qc >= tq, qc - tq, qc)
            st = jnp.where(kj <= qc, st, NEG_INF)
        s_ref[...] = st
        mx_ref[...] = jnp.max(st, axis=0, keepdims=True)

    def consume(sel, blk, s_ref, mx_ref):
        m_prev = m_s[sel]
        m_new = jnp.maximum(m_prev, mx_ref[...])
        alpha = jnp.exp2(m_prev - m_new)
        pt = jnp.exp2(s_ref[...] - m_new)
        l_s[sel] = alpha * l_s[sel] + jnp.sum(pt, axis=0, keepdims=True)
        acc_s[sel] = alpha * acc_s[sel] + jnp.dot(vtb[blk], pt.astype(BF16), preferred_element_type=F32)
        m_s[sel] = m_new

    items = [(0, tiles[0], True), (1, tiles[1], True)]
    for idx in range(nt - 1):
        hi = (idx >= p).astype(jnp.int32)
        items.append((hi, idx - hi * p, False))
    bufs = ((sa, mxa), (sb, mxb))
    scores(*items[0], *bufs[0])
    for j, (sel, blk, _) in enumerate(items):
        if j + 1 < len(items):
            scores(*items[j + 1], *bufs[(j + 1) % 2])
        consume(sel, blk, *bufs[j % 2])

    lam = _diff_lambda(lq1, lk1, lq2, lk2, lam_init)
    for sel, o_ref in enumerate((olo_ref, ohi_ref)):
        a = acc_s[sel] * (1.0 / l_s[sel])
        ot = a[:, :tq] - lam * a[:, tq:]
        ms = jnp.mean(ot * ot, axis=0, keepdims=True)
        ot = ot * lax.rsqrt(ms + SUBLN_EPS) * sw_ref[...] * (1.0 - lam_init)
        o_ref[0] = ot.T.astype(o_ref.dtype)


def _prompt_attention(q, k, v, lam_params, subln_col, lam_init, tq):
    b, t, d = q.shape
    hp = d // V_DIM
    nt = t // tq
    assert nt % 2 == 0
    half = nt // 2
    qlo = pl.BlockSpec((1, tq, V_DIM), lambda bi, h, p: (bi, p, h))
    qhi = pl.BlockSpec((1, tq, V_DIM), lambda bi, h, p: (bi, nt - 1 - p, h))
    ohi = pl.BlockSpec((1, tq, V_DIM), lambda bi, h, p: (bi, half - 1 - p, h))
    kspec = pl.BlockSpec((1, V_DIM, t), lambda bi, h, p: (bi, h, 0))
    vspec = pl.BlockSpec((1, t, V_DIM), lambda bi, h, p: (bi, 0, h))
    lspec = _resident((1, HEAD_DIM))
    stat = pltpu.VMEM((2, 1, 2 * tq), F32)
    o_lo, o_hi = pl.pallas_call(
        functools.partial(_attn_kernel, tq=tq, lam_init=lam_init),
        grid=(b, hp, half),
        in_specs=[qlo, qhi, kspec, vspec, lspec, lspec, lspec, lspec, _resident((V_DIM, 1))],
        out_specs=[qlo, ohi],
        out_shape=[jax.ShapeDtypeStruct((b, t // 2, d), BF16)] * 2,
        scratch_shapes=[pltpu.VMEM((t, V_DIM), BF16), pltpu.VMEM((nt, V_DIM, tq), BF16),
                        pltpu.VMEM((2, V_DIM, 2 * tq), BF16),
                        pltpu.VMEM((tq, 2 * tq), F32), pltpu.VMEM((tq, 2 * tq), F32),
                        pltpu.VMEM((1, 2 * tq), F32), pltpu.VMEM((1, 2 * tq), F32),
                        stat, stat, pltpu.VMEM((2, V_DIM, 2 * tq), F32)],
        compiler_params=_params("parallel", "parallel", "arbitrary"),
        name="prompt_attn",
    )(q, q, k, v, *lam_params, subln_col)
    return jnp.concatenate([o_lo, o_hi], axis=1)


def _sattn_kernel(pt_ref, q_ref, kn_ref, vn_ref, *rest, pps, lam_init, n_new):
    del pt_ref
    k_refs, v_refs = rest[:pps], rest[pps:2 * pps]
    lq1, lk1, lq2, lk2, sw_ref, o_ref, m_s, l_s, acc_s = rest[2 * pps:]
    g = pl.program_id(1)
    ng = pl.num_programs(1)
    d = q_ref.shape[2]
    nc = d // HEAD_DIM
    nh = d // V_DIM
    page = kn_ref.shape[3]
    rows = nc * SUBLANES

    @pl.when(g == 0)
    def _():
        m_s[...] = jnp.full(m_s.shape, NEG_INF, F32)
        l_s[...] = jnp.zeros(l_s.shape, F32)
        acc_s[...] = jnp.zeros(acc_s.shape, F32)

    q = (q_ref[0] * (HEAD_DIM ** -0.5)).astype(BF16)
    qcs = [q[:, c * HEAD_DIM:(c + 1) * HEAD_DIM] for c in range(nc)]

    def update(kv_pairs, mask):
        scores = []
        for kr, _ in kv_pairs:
            s = jnp.concatenate([jnp.dot(qcs[c], kr[c].astype(BF16), preferred_element_type=F32) for c in range(nc)],
                                axis=0)
            if mask is not None:
                s = jnp.where(mask, s, NEG_INF)
            scores.append(s)
        mx = scores[0]
        for s in scores[1:]:
            mx = jnp.maximum(mx, s)
        m_prev = m_s[...]
        m_new = jnp.maximum(m_prev, jnp.max(mx, axis=1, keepdims=True))
        alpha = jnp.exp(m_prev - m_new)
        psum = None
        pv = [None] * nh
        for s, (_, vr) in zip(scores, kv_pairs):
            p = jnp.exp(s - m_new)
            psum = p if psum is None else psum + p
            pb = p.astype(BF16)
            for h in range(nh):
                vh = vr[pl.ds(h, page, stride=nh), :].astype(BF16)
                t = jnp.dot(pb[2 * h * SUBLANES:(2 * h + 2) * SUBLANES, :], vh, preferred_element_type=F32)
                pv[h] = t if pv[h] is None else pv[h] + t
        acc_s[...] = alpha * acc_s[...] + jnp.concatenate(pv, axis=0)
        l_s[...] = alpha * l_s[...] + jnp.sum(psum, axis=1, keepdims=True)
        m_s[...] = m_new

    update(list(zip(k_refs, v_refs)), None)

    @pl.when(g == ng - 1)
    def _():
        kj = lax.broadcasted_iota(jnp.int32, (rows, page), 1)
        slot = lax.broadcasted_iota(jnp.int32, (rows, page), 0) % SUBLANES
        mask = (kj < n_new) & (kj <= slot)
        update([(kn_ref.at[0], vn_ref.at[0])], mask)
        a = acc_s[...] * (1.0 / l_s[...])
        lam = _diff_lambda(lq1, lk1, lq2, lk2, lam_init)
        for h in range(nh):
            a1 = a[(2 * h) * SUBLANES:(2 * h + 1) * SUBLANES, :]
            a2 = a[(2 * h + 1) * SUBLANES:(2 * h + 2) * SUBLANES, :]
            o = a1 - lam * a2
            o_ref[0, :, h * V_DIM:(h + 1) * V_DIM] = _rms(o, sw_ref[...], SUBLN_EPS) * (1.0 - lam_init)


def _sample_attention(q, k_new, v_new, cache_k, cache_v, layer, page_table, lam_params, subln_row, lam_init, pps):
    s, n_new, d = q.shape
    na, n_pool, page, nc, _ = cache_k.shape
    nh = cache_v.shape[3]
    n_pages = page_table.shape[1]
    rows = nc * SUBLANES
    qp = jnp.pad(q, ((0, 0), (0, SUBLANES - n_new), (0, 0)))
    ck = jnp.transpose(cache_k, (0, 1, 3, 4, 2))
    cv = cache_v.reshape(na, n_pool, page * nh, V_DIM)
    knp = jnp.pad(jnp.transpose(k_new.reshape(s, n_new, nc, HEAD_DIM), (0, 2, 3, 1)),
                  ((0, 0), (0, 0), (0, 0), (0, page - n_new)))
    vnp = jnp.pad(v_new, ((0, 0), (0, page - n_new), (0, 0))).reshape(s, page * nh, V_DIM)

    kspec = lambda j: pl.BlockSpec((None, None, nc, HEAD_DIM, page),
                                   lambda si, g, pt: (layer, pt[si, g * pps + j], 0, 0, 0))
    vspec = lambda j: pl.BlockSpec((None, None, page * nh, V_DIM), lambda si, g, pt: (layer, pt[si, g * pps + j], 0, 0))
    seq8 = pl.BlockSpec((1, SUBLANES, d), lambda si, g, pt: (si, 0, 0))
    knspec = pl.BlockSpec((1, nc, HEAD_DIM, page), lambda si, g, pt: (si, 0, 0, 0))
    vnspec = pl.BlockSpec((1, page * nh, V_DIM), lambda si, g, pt: (si, 0, 0))
    small = lambda shape: pl.BlockSpec(shape, lambda si, g, pt: (0,) * len(shape))

    out = pl.pallas_call(
        functools.partial(_sattn_kernel, pps=pps, lam_init=lam_init, n_new=n_new),
        grid_spec=pltpu.PrefetchScalarGridSpec(
            num_scalar_prefetch=1,
            grid=(s, n_pages // pps),
            in_specs=[seq8, knspec, vnspec] + [kspec(j) for j in range(pps)] + [vspec(j) for j in range(pps)]
            + [small((1, HEAD_DIM))] * 4 + [small((1, V_DIM))],
            out_specs=seq8,
            scratch_shapes=[pltpu.VMEM((rows, 1), F32), pltpu.VMEM((rows, 1), F32), pltpu.VMEM((rows, V_DIM), F32)],
        ),
        out_shape=jax.ShapeDtypeStruct((s, SUBLANES, d), F32),
        compiler_params=_params("parallel", "arbitrary"),
        name="sample_attn",
    )(page_table, qp, knp, vnp, *([ck] * pps), *([cv] * pps), *lam_params, subln_row)
    return out[:, :n_new]


def _post_kernel(*refs, ssm_front, final, fc):
    if ssm_front:
        y_ref, z_ref, h_ref, gw_ref, wo_ref = refs[:5]
        rest = refs[5:]
        z = z_ref[...]
        yg = y_ref[...] * (z * jax.nn.sigmoid(z))
        gs = yg.shape[1] // SSM_GROUPS
        mix = jnp.concatenate([_rms(yg[:, g * gs:(g + 1) * gs], gw_ref[:, g * gs:(g + 1) * gs], EPS).astype(BF16)
                               for g in range(SSM_GROUPS)], axis=1)
    else:
        o_ref, h_ref, wo_ref = refs[:3]
        rest = refs[3:]
        mix = o_ref[...].astype(BF16)
    gm_ref, wu_ref, wd_ref, p_ref, gp_ref, wg_ref, wp_ref = rest[:7]
    h = h_ref[...] + jnp.dot(mix, wo_ref[...], preferred_element_type=F32)
    xn = _rms(h, gm_ref[...], EPS).astype(BF16)
    for c in range(wu_ref.shape[1] // fc):
        a = jnp.maximum(jnp.dot(xn, wu_ref[:, c * fc:(c + 1) * fc], preferred_element_type=F32), 0.0)
        h = h + jnp.dot((a * a).astype(BF16), wd_ref[c * fc:(c + 1) * fc, :], preferred_element_type=F32)
    gate = jax.nn.sigmoid(jnp.dot(_rms(h, gp_ref[...], EPS).astype(BF16), wg_ref[...], preferred_element_type=F32))
    out = h + gate * jnp.dot(p_ref[...].astype(BF16), wp_ref[...], preferred_element_type=F32)
    if final:
        fn_ref, out_ref = rest[7:]
        out_ref[...] = _rms(out, fn_ref[...], EPS)
    else:
        (out_ref,) = rest[7:]
        out_ref[...] = out


def _post(mix_in, h, mix_w, mlp_w, ple_in, final_gain, tm):
    m, d = h.shape
    ssm_front = len(mix_in) == 3
    gm, wu, wd = mlp_w
    p, gp, wg, wp = ple_in
    ff, pd = wu.shape[1], p.shape[1]
    rows = lambda width: pl.BlockSpec((tm, width), lambda i: (i, 0))
    if ssm_front:
        y, z, gw = mix_in
        di = y.shape[1]
        ins = [y, z, h, gw, mix_w]
        specs = [rows(di), rows(di), rows(d), _resident((1, di)), _resident((di, d))]
    else:
        ins = [mix_in[0], h, mix_w]
        specs = [rows(d), rows(d), _resident((d, d))]
    ins += [gm, wu, wd, p, gp, wg, wp]
    specs += [_resident((1, d)), _resident((d, ff)), _resident((ff, d)), rows(pd), _resident((1, d)),
              _resident((d, d)), _resident((pd, d))]
    final = final_gain is not None
    if final:
        ins.append(final_gain)
        specs.append(_resident((1, d)))
    return pl.pallas_call(
        functools.partial(_post_kernel, ssm_front=ssm_front, final=final, fc=min(ff, 1024)), grid=(m // tm,),
        in_specs=specs, out_specs=rows(d), out_shape=jax.ShapeDtypeStruct((m, d), F32),
        compiler_params=_params("parallel"), name="post_ssm" if ssm_front else "post_attn",
    )(*ins)


def _inproj_kernel(x_ref, g_ref, w_ref, z_ref, xbc_ref, dt_ref):
    xn = _rms(x_ref[...], g_ref[...], EPS).astype(BF16)
    nz, nx = z_ref.shape[1], xbc_ref.shape[1]
    z_ref[...] = jnp.dot(xn, w_ref[:, :nz], preferred_element_type=F32)
    xbc_ref[...] = jnp.dot(xn, w_ref[:, nz:nz + nx], preferred_element_type=F32)
    dt_ref[...] = jnp.dot(xn, w_ref[:, nz + nx:], preferred_element_type=F32)


def _inproj(x, gain, w, d_inner, conv_dim, tm):
    m, d = x.shape
    n = w.shape[1]
    ndt = n - d_inner - conv_dim
    row = lambda width: pl.BlockSpec((tm, width), lambda i: (i, 0))
    return pl.pallas_call(
        _inproj_kernel, grid=(m // tm,),
        in_specs=[row(d), _resident((1, d)), _resident((d, n))],
        out_specs=[row(d_inner), row(conv_dim), row(ndt)],
        out_shape=[jax.ShapeDtypeStruct((m, d_inner), F32), jax.ShapeDtypeStruct((m, conv_dim), F32),
                   jax.ShapeDtypeStruct((m, ndt), F32)],
        compiler_params=_params("parallel"), name="inproj",
    )(x, gain, w)


def _inproj_conv_kernel(x_ref, g_ref, w_ref, cw_ref, cb_ref, z_ref, xs_ref, bc_ref, dt_ref, tail_ref, hist_s,
                        *, tiles_per_seq, cblk):
    i = pl.program_id(0)
    tm = x_ref.shape[0]
    nz, nx, nbc = z_ref.shape[1], xs_ref.shape[1], bc_ref.shape[1]
    cd = nx + nbc
    xn = _rms(x_ref[...], g_ref[...], EPS).astype(BF16)

    @pl.when(i % tiles_per_seq == 0)
    def _():
        hist_s[...] = jnp.zeros(hist_s.shape, F32)

    nblk = cd // cblk
    zb = [(nz // LANES * k // nblk) * LANES for k in range(nblk + 1)]
    for k in range(nblk):
        c0 = k * cblk
        cols = slice(c0, c0 + cblk)
        raw = jnp.dot(xn, w_ref[:, nz + c0:nz + c0 + cblk], preferred_element_type=F32)
        if zb[k + 1] > zb[k]:
            z_ref[:, zb[k]:zb[k + 1]] = jnp.dot(xn, w_ref[:, zb[k]:zb[k + 1]], preferred_element_type=F32)
        act = _conv_silu(jnp.concatenate([hist_s[:, cols], raw], axis=0), cw_ref, cb_ref, cols)
        hist_s[:, cols] = raw[tm - SUBLANES:, :]
        if c0 < nx:
            xs_ref[:, cols] = act
        else:
            bc_ref[:, c0 - nx:c0 - nx + cblk] = act
    dt_ref[...] = jnp.dot(xn, w_ref[:, nz + cd:], preferred_element_type=F32)
    tail_ref[0] = hist_s[...]


def _inproj_conv(x, gain, w, cw, cb, d_inner, t, tm):
    m, d = x.shape
    n, cd = w.shape[1], cw.shape[1]
    nbc, ndt = cd - d_inner, n - d_inner - cd
    cblk = 512
    assert d_inner % cblk == 0 and nbc % cblk == 0 and t % tm == 0
    tps = t // tm
    row = lambda width: pl.BlockSpec((tm, width), lambda i: (i, 0))
    sds = lambda width: jax.ShapeDtypeStruct((m, width), F32)
    return pl.pallas_call(
        functools.partial(_inproj_conv_kernel, tiles_per_seq=tps, cblk=cblk), grid=(m // tm,),
        in_specs=[row(d), _resident((1, d)), _resident((d, n)), _resident((CONV_K, cd)), _resident((1, cd))],
        out_specs=[row(d_inner), row(d_inner), row(nbc), row(ndt),
                   pl.BlockSpec((1, SUBLANES, cd), lambda i: (i // tps, 0, 0))],
        out_shape=[sds(d_inner), sds(d_inner), sds(nbc), sds(ndt), jax.ShapeDtypeStruct((m // t, SUBLANES, cd), F32)],
        scratch_shapes=[pltpu.VMEM((SUBLANES, cd), F32)],
        compiler_params=_params("arbitrary"), name="inproj_conv",
    )(x, gain, w, cw, cb)


def _softplus(x):
    return jnp.maximum(x, 0.0) + jnp.log1p(jnp.exp(-jnp.abs(x)))


def _conv_silu(w, cw_ref, cb_ref, cols):
    conv = cb_ref[:, cols] + cw_ref[CONV_K - 1:CONV_K, cols] * w[SUBLANES:]
    for s in range(1, CONV_K):
        tap = CONV_K - 1 - s
        conv = conv + cw_ref[tap:tap + 1, cols] * pltpu.roll(w, s, 0)[SUBLANES:]
    return conv * jax.nn.sigmoid(conv)


def _ssd_conv_kernel(xbc_ref, dt_ref, c0_ref, h0_ref, cw_ref, cb_ref, dtb_ref, alog_ref, dsk_ref,
                     y_ref, hout_ref, ht_s, win_s, act_s, *, valid, n_heads):
    L = xbc_ref.shape[1]
    di = n_heads * SSM_HEAD_DIM

    @pl.when(pl.program_id(1) == 0)
    def _():
        win_s[0:SUBLANES, :] = c0_ref[0]

    win_s[SUBLANES:SUBLANES + L, :] = xbc_ref[0]
    act_s[...] = _conv_silu(win_s[0:L + SUBLANES, :], cw_ref, cb_ref, slice(None))
    win_s[0:SUBLANES, :] = win_s[L:L + SUBLANES, :]
    _ssd_chunk(lambda cols: act_s[:, cols], lambda s: act_s[:, di + s:di + s + D_STATE],
               dt_ref, h0_ref, dtb_ref, alog_ref, dsk_ref, y_ref, hout_ref, ht_s, valid=valid, n_heads=n_heads)


def _ssd_act_kernel(xs_ref, bc_ref, dt_ref, h0_ref, dtb_ref, alog_ref, dsk_ref, y_ref, hout_ref, ht_s,
                    *, valid, n_heads):
    _ssd_chunk(lambda cols: xs_ref[0, :, cols], lambda s: bc_ref[0, :, s:s + D_STATE],
               dt_ref, h0_ref, dtb_ref, alog_ref, dsk_ref, y_ref, hout_ref, ht_s, valid=valid, n_heads=n_heads)


def _ssd_chunk(x_of, bc_of, dt_ref, h0_ref, dtb_ref, alog_ref, dsk_ref, y_ref, hout_ref, ht_s, *, valid, n_heads):
    c = pl.program_id(1)
    nchunks = pl.num_programs(1)
    L = dt_ref.shape[1]
    hpg = n_heads // SSM_GROUPS
    gw = hpg * SSM_HEAD_DIM

    @pl.when(c == 0)
    def _():
        ht_s[...] = h0_ref[0].T

    dt = _softplus(dt_ref[0] + dtb_ref[...])
    lane = lax.broadcasted_iota(jnp.int32, dt.shape, 1)
    ok = lane < n_heads
    if valid < L:
        ok = ok & (lax.broadcasted_iota(jnp.int32, dt.shape, 0) < valid)
    dt = jnp.where(ok, dt, 0.0)
    dta = dt * (-jnp.exp(alog_ref[...]))
    tril = lax.broadcasted_iota(jnp.int32, (L, L), 0) >= lax.broadcasted_iota(jnp.int32, (L, L), 1)
    trilb = jnp.where(tril, 1.0, 0.0).astype(BF16)
    hi = dta.astype(BF16)
    r1 = dta - hi.astype(F32)
    mid = r1.astype(BF16)
    lo = (r1 - mid.astype(F32)).astype(BF16)
    acs = (jnp.dot(trilb, hi, preferred_element_type=F32) + jnp.dot(trilb, mid, preferred_element_type=F32)
           + jnp.dot(trilb, lo, preferred_element_type=F32))
    acs_t = acs.T
    last = acs[L - 1:L, :]
    pw = 2 * SSM_HEAD_DIM
    first = lax.broadcasted_iota(jnp.int32, (L, pw), 1) < SSM_HEAD_DIM

    def pair(a, h0, rows):
        return jnp.where(first[:rows], jnp.broadcast_to(a[:, h0:h0 + 1], (rows, pw)),
                         jnp.broadcast_to(a[:, h0 + 1:h0 + 2], (rows, pw)))

    for g in range(SSM_GROUPS):
        bg_t = bc_of(g * D_STATE).T.astype(BF16)
        cg = bc_of((SSM_GROUPS + g) * D_STATE).astype(BF16)
        cbm = jnp.dot(cg, bg_t, preferred_element_type=F32)
        hg = ht_s[:, g * gw:(g + 1) * gw]
        yin = jnp.dot(cg, hg.astype(BF16), preferred_element_type=F32)
        xw, el = [], []
        for pr in range(hpg // 2):
            h0 = g * hpg + 2 * pr
            cols = slice(h0 * SSM_HEAD_DIM, (h0 + 2) * SSM_HEAD_DIM)
            x2 = x_of(cols)
            acol = pair(acs, h0, L)
            last2 = pair(last, h0, 1)
            xdt = x2 * pair(dt, h0, L)
            xdtb = xdt.astype(BF16)
            ys = []
            for r in range(2):
                col = jnp.broadcast_to(acs[:, h0 + r:h0 + r + 1], (L, L))
                dec = jnp.exp(jnp.where(tril, col - acs_t[h0 + r:h0 + r + 1, :], NEG_INF))
                ys.append(jnp.dot((cbm * dec).astype(BF16), xdtb, preferred_element_type=F32))
            y_ref[0, :, cols] = (jnp.where(first, ys[0], ys[1]) + yin[:, pr * pw:(pr + 1) * pw] * jnp.exp(acol)
                                 + dsk_ref[:, cols] * x2)
            xw.append((xdt * jnp.exp(last2 - acol)).astype(BF16))
            el.append(jnp.exp(last2))
        xw = jnp.concatenate(xw, axis=1) if len(xw) > 1 else xw[0]
        el = jnp.concatenate(el, axis=1) if len(el) > 1 else el[0]
        ht_s[:, g * gw:(g + 1) * gw] = hg * el + jnp.dot(bg_t, xw, preferred_element_type=F32)

    @pl.when(c == nchunks - 1)
    def _():
        hout_ref[0] = ht_s[...].T


def _ssd(x_in, dt_raw, h0, dtb, alog, dsk, valid, conv=None):
    b, t, _ = dt_raw.shape
    di = h0.shape[1]
    L = min(CHUNK, t)
    chunk = lambda width: pl.BlockSpec((1, L, width), lambda bi, c: (bi, c, 0))
    per_b = lambda rows, width: pl.BlockSpec((1, rows, width), lambda bi, c: (bi, 0, 0))
    small = lambda shape: pl.BlockSpec(shape, lambda bi, c: (0,) * len(shape))
    tail_specs = [small((1, LANES)), small((1, LANES)), small((1, di))]
    scratch = [pltpu.VMEM((D_STATE, di), F32)]
    if conv is not None:
        (xbc,), (c0, cw, cb) = x_in, conv
        cd = xbc.shape[2]
        body = _ssd_conv_kernel
        ins = [xbc, dt_raw, c0, h0, cw, cb, dtb, alog, dsk]
        specs = [chunk(cd), chunk(LANES), per_b(SUBLANES, cd), per_b(di, D_STATE), small((CONV_K, cd)),
                 small((1, cd))] + tail_specs
        scratch += [pltpu.VMEM((L + 2 * SUBLANES, cd), F32), pltpu.VMEM((L, cd), F32)]
    else:
        xs, bc = x_in
        body = _ssd_act_kernel
        ins = [xs, bc, dt_raw, h0, dtb, alog, dsk]
        specs = [chunk(di), chunk(bc.shape[2]), chunk(LANES), per_b(di, D_STATE)] + tail_specs
    return pl.pallas_call(
        functools.partial(body, valid=valid, n_heads=di // SSM_HEAD_DIM), grid=(b, t // L),
        in_specs=specs, out_specs=[chunk(di), per_b(di, D_STATE)],
        out_shape=[jax.ShapeDtypeStruct((b, t, di), F32), jax.ShapeDtypeStruct((b, di, D_STATE), F32)],
        scratch_shapes=scratch,
        compiler_params=_params("parallel", "arbitrary"), name="ssd",
    )(*ins)


def _tile(m, want):
    return want if m % want == 0 else m


def kernel(x_prompt, x_sample, cache_k, cache_v, state_ssm, state_conv, page_table, p_prompt, p_sample, attn_norm, w_qkv, lambda_q1, lambda_k1, lambda_q2, lambda_k2, subln_w, w_o, ssm_norm, w_in, conv_w, conv_b, dt_bias, a_log, d_skip, gnorm_w, w_out, mlp_norm, w_up, w_down, ple_norm, w_ple_gate, w_ple_proj, final_norm):
    bp, tp, d = x_prompt.shape
    bs, ts, _ = x_sample.shape
    depth = mlp_norm.shape[0]
    past_len = page_table.shape[1] * cache_k.shape[2]
    n_heads_ssm = dt_bias.shape[1] if dt_bias.ndim == 2 else 0
    mp, msz = bp * tp, bs * ts
    tm_p = _tile(mp, 512)
    tm_s = msz
    bf = lambda a: a.astype(BF16)
    row = lambda a: a.reshape(1, -1)

    hp = x_prompt.reshape(mp, d)
    hs = x_sample.reshape(msz, d)
    tabs_p = _rope_tables(jnp.arange(tp, dtype=jnp.int32))
    tabs_s = _rope_tables(jnp.tile(past_len + jnp.arange(ts, dtype=jnp.int32), bs))

    kp_l, vp_l, ks_l, vs_l, sp_l, cp_l, ss_l, cs_l = [], [], [], [], [], [], [], []
    for i in range(depth):
        mlp_w = (row(mlp_norm[i]), bf(w_up[i]), bf(w_down[i]))
        ple_w = (row(ple_norm[i]), bf(w_ple_gate[i]), bf(w_ple_proj[i]))
        ple_p = (p_prompt[i].reshape(mp, -1),) + ple_w
        ple_s = (p_sample[i].reshape(msz, -1),) + ple_w
        fin = row(final_norm) if i == depth - 1 else None
        if i % 2 == 0:
            a = i // 2
            lam_init = 0.8 - 0.6 * math.exp(-0.3 * i)
            lam_params = [row(lambda_q1[a]), row(lambda_k1[a]), row(lambda_q2[a]), row(lambda_k2[a])]
            wqkv, wo, gain = bf(w_qkv[a]), bf(w_o[a]), row(attn_norm[a])
            q, kt, v = _qkv(hp, gain, wqkv, tabs_p, _tile(tp, 512), True)
            o = _prompt_attention(q.reshape(bp, tp, d), kt, v.reshape(bp, tp, d), lam_params,
                                  subln_w[a].reshape(V_DIM, 1), lam_init, _tile(tp, 512))
            hp = _post((o.reshape(mp, d),), hp, wo, mlp_w, ple_p, fin, tm_p)
            kp_l.append(jnp.transpose(kt.reshape(bp, d // HEAD_DIM, HEAD_DIM, tp), (0, 3, 1, 2)))
            vp_l.append(v.reshape(bp, tp, d // V_DIM, V_DIM))
            q, k, v = _qkv(hs, gain, wqkv, tabs_s, tm_s, False)
            o = _sample_attention(q.reshape(bs, ts, d), k.reshape(bs, ts, d), v.reshape(bs, ts, d), cache_k, cache_v,
                                  a, page_table, lam_params, row(subln_w[a]), lam_init, 8)
            hs = _post((o.reshape(msz, d),), hs, wo, mlp_w, ple_s, fin, tm_s)
            ks_l.append(k.reshape(bs, ts, d // HEAD_DIM, HEAD_DIM))
            vs_l.append(v.reshape(bs, ts, d // V_DIM, V_DIM))
        else:
            s = i // 2
            nh = n_heads_ssm
            di = nh * SSM_HEAD_DIM
            cd = conv_w.shape[2]
            pad_h = LANES - nh
            win = bf(jnp.pad(w_in[s], ((0, 0), (0, pad_h))))
            dtb = jnp.pad(dt_bias[s], (0, pad_h)).reshape(1, LANES)
            alog = jnp.pad(a_log[s], (0, pad_h)).reshape(1, LANES)
            dsk = jnp.repeat(d_skip[s], SSM_HEAD_DIM).reshape(1, di)
            gain, gw, wout = row(ssm_norm[s]), row(gnorm_w[s]), bf(w_out[s])
            cw, cb = conv_w[s], row(conv_b[s])
            z, xs, bca, dtr, tail = _inproj_conv(hp, gain, win, cw, cb, di, tp, _tile(tp, 256))
            y, st = _ssd((xs.reshape(bp, tp, di), bca.reshape(bp, tp, cd - di)), dtr.reshape(bp, tp, LANES),
                         jnp.zeros((bp, di, D_STATE), F32), dtb, alog, dsk, CHUNK)
            hp = _post((y.reshape(mp, di), z, gw), hp, wout, mlp_w, ple_p, fin, _tile(mp, 256))
            sp_l.append(st.reshape(bp, nh, SSM_HEAD_DIM, D_STATE))
            cp_l.append(tail[:, SUBLANES - (CONV_K - 1):])
            z, xbc, dtr = _inproj(hs, gain, win, di, cd, tm_s)
            xbc3 = xbc.reshape(bs, ts, cd)
            padt = ((0, 0), (0, CHUNK - ts), (0, 0))
            c0 = jnp.pad(state_conv[s], ((0, 0), (SUBLANES - (CONV_K - 1), 0), (0, 0)))
            y, st = _ssd((jnp.pad(xbc3, padt),), jnp.pad(dtr.reshape(bs, ts, LANES), padt),
                         state_ssm[s].reshape(bs, di, D_STATE), dtb, alog, dsk, ts, conv=(c0, cw, cb))
            hs = _post((y[:, :ts].reshape(msz, di), z, gw), hs, wout, mlp_w, ple_s, fin, tm_s)
            ss_l.append(st.reshape(bs, nh, SSM_HEAD_DIM, D_STATE))
            cs_l.append(jnp.concatenate([state_conv[s], xbc3], axis=1)[:, ts:])

    return (hp.reshape(bp, tp, d), hs.reshape(bs, ts, d), jnp.stack(kp_l), jnp.stack(vp_l), jnp.stack(ks_l),
            jnp.stack(vs_l), jnp.stack(sp_l), jnp.stack(cp_l), jnp.stack(ss_l), jnp.stack(cs_l))
```

```python
import functools
import math

import jax
import jax.numpy as jnp
from jax import lax
from jax.experimental import pallas as pl
from jax.experimental.pallas import tpu as pltpu

F32 = jnp.float32
BF16 = jnp.bfloat16

HEAD_DIM = 64
V_DIM = 2 * HEAD_DIM
ROT_DIM = HEAD_DIM // 4
ROPE_THETA = 500000.0
SUBLN_EPS = 1e-5
EPS = 1e-6
NEG_INF = -1e30
LOG2E = 1.4426950408889634
SSM_HEAD_DIM = 64
SSM_GROUPS = 8
D_STATE = 128
CONV_K = 4
CHUNK = 128
LANES = 128
SUBLANES = 8
VMEM_LIMIT = 56 * 1024 * 1024


def _params(*sem):
    return pltpu.CompilerParams(dimension_semantics=sem, vmem_limit_bytes=VMEM_LIMIT)


def _resident(shape):
    return pl.BlockSpec(shape, lambda *_: (0,) * len(shape), pipeline_mode=pl.Buffered(1))


def _rms(x, g, eps):
    ms = jnp.mean(x * x, axis=-1, keepdims=True)
    return x * lax.rsqrt(ms + eps) * g


def _row_to_col(row):
    n = row.shape[1]
    eye = lax.broadcasted_iota(jnp.int32, (n, n), 0) == lax.broadcasted_iota(jnp.int32, (n, n), 1)
    return jnp.sum(jnp.where(eye, jnp.broadcast_to(row, (n, n)), 0.0), axis=1, keepdims=True)


def _diff_lambda(lq1, lk1, lq2, lk2, lam_init):
    s1 = jnp.sum(lq1[...] * lk1[...], axis=1, keepdims=True)
    s2 = jnp.sum(lq2[...] * lk2[...], axis=1, keepdims=True)
    return jnp.exp(s1) - jnp.exp(s2) + lam_init


def _qkv_kernel(x_ref, g_ref, w_ref, c_ref, s1_ref, s2_ref, q_ref, k_ref, v_ref, *, k_transposed):
    d = x_ref.shape[1]
    xn = _rms(x_ref[...], g_ref[...], EPS).astype(BF16)
    c, s1, s2 = c_ref[...], s1_ref[...], s2_ref[...]
    for part in range(2):
        y = jnp.dot(xn, w_ref[:, part * d:(part + 1) * d], preferred_element_type=F32)
        for hb in range(d // LANES):
            cols = slice(hb * LANES, (hb + 1) * LANES)
            blk = y[:, cols]
            rot = blk * c + pltpu.roll(blk, LANES - ROT_DIM // 2, 1) * s1 + pltpu.roll(blk, ROT_DIM // 2, 1) * s2
            if not k_transposed:
                (q_ref, k_ref)[part][:, cols] = rot
            elif part == 0:
                q_ref[0, cols, :] = (rot * (HEAD_DIM ** -0.5 * LOG2E)).T.astype(q_ref.dtype)
            else:
                k_ref[0, cols, :] = rot.T
    v_ref[...] = jnp.dot(xn, w_ref[:, 2 * d:3 * d], preferred_element_type=F32)


def _rope_tables(pos):
    half = ROT_DIM // 2
    inv_freq = jnp.power(ROPE_THETA, -jnp.arange(half, dtype=F32) * (2.0 / ROT_DIM))
    ang = pos.astype(F32)[:, None] * inv_freq[None, :]
    cos, sin = jnp.cos(ang), jnp.sin(ang)
    t = pos.shape[0]
    z8 = jnp.zeros((t, half), F32)
    rest0 = jnp.zeros((t, HEAD_DIM - ROT_DIM), F32)
    c = jnp.concatenate([cos, cos, jnp.ones((t, HEAD_DIM - ROT_DIM), F32)], axis=1)
    s1 = jnp.concatenate([-sin, z8, rest0], axis=1)
    s2 = jnp.concatenate([z8, sin, rest0], axis=1)
    rep = LANES // HEAD_DIM
    return tuple(jnp.tile(a, (1, rep)) for a in (c, s1, s2))


def _qkv(x, gain, w, tabs, tm, k_transposed):
    m, d = x.shape
    t = tabs[0].shape[0]
    nt = t // tm
    row = pl.BlockSpec((tm, d), lambda i: (i, 0))
    tab = pl.BlockSpec((tm, LANES), lambda i: (i % nt, 0))
    qspec = kspec = row
    qshape = kshape = jax.ShapeDtypeStruct((m, d), F32)
    if k_transposed:
        qspec = kspec = pl.BlockSpec((1, d, tm), lambda i: (i // nt, 0, i % nt))
        qshape = jax.ShapeDtypeStruct((m // t, d, t), BF16)
        kshape = jax.ShapeDtypeStruct((m // t, d, t), F32)
    return pl.pallas_call(
        functools.partial(_qkv_kernel, k_transposed=k_transposed),
        grid=(m // tm,),
        in_specs=[row, _resident((1, d)), _resident((d, 3 * d)), tab, tab, tab],
        out_specs=[qspec, kspec, row],
        out_shape=[qshape, kshape, jax.ShapeDtypeStruct((m, d), F32)],
        compiler_params=_params("parallel"),
        name="qkv",
    )(x, gain, w, *tabs)


def _attn_kernel(qlo_ref, qhi_ref, k_ref, v_ref, lq1, lk1, lq2, lk2, sw_ref, olo_ref, ohi_ref,
                 kb, vtb, qbd_s, sa, sb, mxa, mxb, m_s, l_s, acc_s, *, tq, lam_init):
    p = pl.program_id(2)
    nt = vtb.shape[0]
    tiles = (p, nt - 1 - p)

    @pl.when(p == 0)
    def _():
        for c in range(nt):
            kb[c * tq:(c + 1) * tq, :] = k_ref[0, :, c * tq:(c + 1) * tq].T.astype(BF16)
            vtb[c] = v_ref[0, c * tq:(c + 1) * tq, :].T.astype(BF16)

    for sel, q_ref in enumerate((qlo_ref, qhi_ref)):
        qt = q_ref[0]
        rowi = lax.broadcasted_iota(jnp.int32, qt.shape, 0)
        zero = jnp.zeros_like(qt)
        qbd_s[sel] = jnp.concatenate([jnp.where(rowi < HEAD_DIM, qt, zero), jnp.where(rowi >= HEAD_DIM, qt, zero)],
                                     axis=1)
    m_s[...] = jnp.full(m_s.shape, NEG_INF, F32)
    l_s[...] = jnp.zeros(l_s.shape, F32)
    acc_s[...] = jnp.zeros(acc_s.shape, F32)

    def scores(sel, blk, masked, s_ref, mx_ref):
        start = pl.multiple_of(blk * tq, tq)
        st = jnp.dot(kb[pl.ds(start, tq), :], qbd_s[sel], preferred_element_type=F32)
        if masked:
            kj = lax.broadcasted_iota(jnp.int32, st.shape, 0)
            qc = lax.broadcasted_iota(jnp.int32, st.shape, 1)
            qc = jnp.where(qc >= tq, qc - tq, qc)
            st = jnp.where(kj <= qc, st, NEG_INF)
        s_ref[...] = st
        mx_ref[...] = jnp.max(st, axis=0, keepdims=True)

    def consume(sel, blk, s_ref, mx_ref):
        m_prev = m_s[sel]
        m_new = jnp.maximum(m_prev, mx_ref[...])
        alpha = jnp.exp2(m_prev - m_new)
        pt = jnp.exp2(s_ref[...] - m_new)
        l_s[sel] = alpha * l_s[sel] + jnp.sum(pt, axis=0, keepdims=True)
        acc_s[sel] = alpha * acc_s[sel] + jnp.dot(vtb[blk], pt.astype(BF16), preferred_element_type=F32)
        m_s[sel] = m_new

    lam = _diff_lambda(lq1, lk1, lq2, lk2, lam_init)

    def finalize(sel, o_ref):
        a = acc_s[sel] * (1.0 / l_s[sel])
        ot = a[:, :tq] - lam * a[:, tq:]
        ms = jnp.mean(ot * ot, axis=0, keepdims=True)
        ot = ot * lax.rsqrt(ms + SUBLN_EPS) * sw_ref[...] * (1.0 - lam_init)
        o_ref[0] = ot.T.astype(o_ref.dtype)

    items = [(0, tiles[0], True), (1, tiles[1], True)]
    for idx in range(nt - 1):
        hi = (idx >= p).astype(jnp.int32)
        items.append((hi, idx - hi * p, False))
    bufs = ((sa, mxa), (sb, mxb))
    scores(*items[0], *bufs[0])
    for j, (sel, blk, _) in enumerate(items):
        if j + 1 < len(items):
            scores(*items[j + 1], *bufs[(j + 1) % 2])
        consume(sel, blk, *bufs[j % 2])
    finalize(0, olo_ref)
    finalize(1, ohi_ref)


def _prompt_attention(q, k, v, lam_params, subln_col, lam_init, tq):
    b, t, d = v.shape
    hp = d // V_DIM
    nt = t // tq
    assert nt % 2 == 0
    half = nt // 2
    qlo = pl.BlockSpec((1, V_DIM, tq), lambda bi, h, p: (bi, h, p))
    qhi = pl.BlockSpec((1, V_DIM, tq), lambda bi, h, p: (bi, h, nt - 1 - p))
    olo = pl.BlockSpec((1, tq, V_DIM), lambda bi, h, p: (bi, p, h))
    ohi = pl.BlockSpec((1, tq, V_DIM), lambda bi, h, p: (bi, half - 1 - p, h))
    kspec = pl.BlockSpec((1, V_DIM, t), lambda bi, h, p: (bi, h, 0))
    vspec = pl.BlockSpec((1, t, V_DIM), lambda bi, h, p: (bi, 0, h))
    lspec = _resident((1, HEAD_DIM))
    stat = pltpu.VMEM((2, 1, 2 * tq), F32)
    return pl.pallas_call(
        functools.partial(_attn_kernel, tq=tq, lam_init=lam_init),
        grid=(b, hp, half),
        in_specs=[qlo, qhi, kspec, vspec, lspec, lspec, lspec, lspec, _resident((V_DIM, 1))],
        out_specs=[olo, ohi],
        out_shape=[jax.ShapeDtypeStruct((b, t // 2, d), BF16)] * 2,
        scratch_shapes=[pltpu.VMEM((t, V_DIM), BF16), pltpu.VMEM((nt, V_DIM, tq), BF16),
                        pltpu.VMEM((2, V_DIM, 2 * tq), BF16),
                        pltpu.VMEM((tq, 2 * tq), F32), pltpu.VMEM((tq, 2 * tq), F32),
                        pltpu.VMEM((1, 2 * tq), F32), pltpu.VMEM((1, 2 * tq), F32),
                        stat, stat, pltpu.VMEM((2, V_DIM, 2 * tq), F32)],
        compiler_params=_params("parallel", "parallel", "arbitrary"),
        name="prompt_attn",
    )(q, q, k, v, *lam_params, subln_col)


def _sattn_kernel(pt_ref, q_ref, kn_ref, vn_ref, *rest, pps, lam_init, n_new):
    del pt_ref
    k_refs, v_refs = rest[:pps], rest[pps:2 * pps]
    lq1, lk1, lq2, lk2, sw_ref, o_ref, m_s, l_s, acc_s = rest[2 * pps:]
    g = pl.program_id(1)
    ng = pl.num_programs(1)
    d = q_ref.shape[2]
    nc = d // HEAD_DIM
    nh = d // V_DIM
    page = kn_ref.shape[3]
    rows = nc * SUBLANES

    @pl.when(g == 0)
    def _():
        m_s[...] = jnp.full(m_s.shape, NEG_INF, F32)
        l_s[...] = jnp.zeros(l_s.shape, F32)
        acc_s[...] = jnp.zeros(acc_s.shape, F32)

    q = (q_ref[0] * (HEAD_DIM ** -0.5)).astype(BF16)
    qcs = [q[:, c * HEAD_DIM:(c + 1) * HEAD_DIM] for c in range(nc)]

    def update(kv_pairs, mask):
        scores = []
        for kr, _ in kv_pairs:
            s = jnp.concatenate([jnp.dot(qcs[c], kr[c].astype(BF16), preferred_element_type=F32) for c in range(nc)],
                                axis=0)
            if mask is not None:
                s = jnp.where(mask, s, NEG_INF)
            scores.append(s)
        mx = scores[0]
        for s in scores[1:]:
            mx = jnp.maximum(mx, s)
        m_prev = m_s[...]
        m_new = jnp.maximum(m_prev, jnp.max(mx, axis=1, keepdims=True))
        alpha = jnp.exp(m_prev - m_new)
        psum = None
        pv = [None] * nh
        for s, (_, vr) in zip(scores, kv_pairs):
            p = jnp.exp(s - m_new)
            psum = p if psum is None else psum + p
            pb = p.astype(BF16)
            for h in range(nh):
                vh = vr[pl.ds(h, page, stride=nh), :].astype(BF16)
                t = jnp.dot(pb[2 * h * SUBLANES:(2 * h + 2) * SUBLANES, :], vh, preferred_element_type=F32)
                pv[h] = t if pv[h] is None else pv[h] + t
        acc_s[...] = alpha * acc_s[...] + jnp.concatenate(pv, axis=0)
        l_s[...] = alpha * l_s[...] + jnp.sum(psum, axis=1, keepdims=True)
        m_s[...] = m_new

    update(list(zip(k_refs, v_refs)), None)

    @pl.when(g == ng - 1)
    def _():
        kj = lax.broadcasted_iota(jnp.int32, (rows, page), 1)
        slot = lax.broadcasted_iota(jnp.int32, (rows, page), 0) % SUBLANES
        mask = (kj < n_new) & (kj <= slot)
        update([(kn_ref.at[0], vn_ref.at[0])], mask)
        a = acc_s[...] * (1.0 / l_s[...])
        lam = _diff_lambda(lq1, lk1, lq2, lk2, lam_init)
        for h in range(nh):
            a1 = a[(2 * h) * SUBLANES:(2 * h + 1) * SUBLANES, :]
            a2 = a[(2 * h + 1) * SUBLANES:(2 * h + 2) * SUBLANES, :]
            o = a1 - lam * a2
            o_ref[0, :, h * V_DIM:(h + 1) * V_DIM] = _rms(o, sw_ref[...], SUBLN_EPS) * (1.0 - lam_init)


def _sample_attention(q, k_new, v_new, cache_k, cache_v, layer, page_table, lam_params, subln_row, lam_init, pps):
    s, n_new, d = q.shape
    na, n_pool, page, nc, _ = cache_k.shape
    nh = cache_v.shape[3]
    n_pages = page_table.shape[1]
    rows = nc * SUBLANES
    qp = jnp.pad(q, ((0, 0), (0, SUBLANES - n_new), (0, 0)))
    ck = jnp.transpose(cache_k, (0, 1, 3, 4, 2))
    cv = cache_v.reshape(na, n_pool, page * nh, V_DIM)
    knp = jnp.pad(jnp.transpose(k_new.reshape(s, n_new, nc, HEAD_DIM), (0, 2, 3, 1)),
                  ((0, 0), (0, 0), (0, 0), (0, page - n_new)))
    vnp = jnp.pad(v_new, ((0, 0), (0, page - n_new), (0, 0))).reshape(s, page * nh, V_DIM)

    kspec = lambda j: pl.BlockSpec((None, None, nc, HEAD_DIM, page),
                                   lambda si, g, pt: (layer, pt[si, g * pps + j], 0, 0, 0))
    vspec = lambda j: pl.BlockSpec((None, None, page * nh, V_DIM), lambda si, g, pt: (layer, pt[si, g * pps + j], 0, 0))
    seq8 = pl.BlockSpec((1, SUBLANES, d), lambda si, g, pt: (si, 0, 0))
    knspec = pl.BlockSpec((1, nc, HEAD_DIM, page), lambda si, g, pt: (si, 0, 0, 0))
    vnspec = pl.BlockSpec((1, page * nh, V_DIM), lambda si, g, pt: (si, 0, 0))
    small = lambda shape: pl.BlockSpec(shape, lambda si, g, pt: (0,) * len(shape))

    out = pl.pallas_call(
        functools.partial(_sattn_kernel, pps=pps, lam_init=lam_init, n_new=n_new),
        grid_spec=pltpu.PrefetchScalarGridSpec(
            num_scalar_prefetch=1,
            grid=(s, n_pages // pps),
            in_specs=[seq8, knspec, vnspec] + [kspec(j) for j in range(pps)] + [vspec(j) for j in range(pps)]
            + [small((1, HEAD_DIM))] * 4 + [small((1, V_DIM))],
            out_specs=seq8,
            scratch_shapes=[pltpu.VMEM((rows, 1), F32), pltpu.VMEM((rows, 1), F32), pltpu.VMEM((rows, V_DIM), F32)],
        ),
        out_shape=jax.ShapeDtypeStruct((s, SUBLANES, d), F32),
        compiler_params=_params("parallel", "arbitrary"),
        name="sample_attn",
    )(page_table, qp, knp, vnp, *([ck] * pps), *([cv] * pps), *lam_params, subln_row)
    return out[:, :n_new]


def _post_kernel(*refs, ssm_front, half_tiles, final, fc):
    if ssm_front:
        y_ref, z_ref, h_ref, gw_ref, wo_ref = refs[:5]
        rest = refs[5:]
        z = z_ref[...]
        yg = y_ref[...] * (z * jax.nn.sigmoid(z))
        gs = yg.shape[1] // SSM_GROUPS
        mix = jnp.concatenate([_rms(yg[:, g * gs:(g + 1) * gs], gw_ref[:, g * gs:(g + 1) * gs], EPS).astype(BF16)
                               for g in range(SSM_GROUPS)], axis=1)
    elif half_tiles:
        olo_ref, ohi_ref, h_ref, wo_ref = refs[:4]
        rest = refs[4:]
        first_half = pl.program_id(0) % (2 * half_tiles) < half_tiles
        mix = jnp.where(first_half, olo_ref[...], ohi_ref[...]).astype(BF16)
    else:
        o_ref, h_ref, wo_ref = refs[:3]
        rest = refs[3:]
        mix = o_ref[...].astype(BF16)
    gm_ref, wu_ref, wd_ref, p_ref, gp_ref, wg_ref, wp_ref = rest[:7]
    h = h_ref[...] + jnp.dot(mix, wo_ref[...], preferred_element_type=F32)
    xn = _rms(h, gm_ref[...], EPS).astype(BF16)
    for c in range(wu_ref.shape[1] // fc):
        a = jnp.maximum(jnp.dot(xn, wu_ref[:, c * fc:(c + 1) * fc], preferred_element_type=F32), 0.0)
        h = h + jnp.dot((a * a).astype(BF16), wd_ref[c * fc:(c + 1) * fc, :], preferred_element_type=F32)
    gate = jax.nn.sigmoid(jnp.dot(_rms(h, gp_ref[...], EPS).astype(BF16), wg_ref[...], preferred_element_type=F32))
    out = h + gate * jnp.dot(p_ref[...].astype(BF16), wp_ref[...], preferred_element_type=F32)
    if final:
        fn_ref, out_ref = rest[7:]
        out_ref[...] = _rms(out, fn_ref[...], EPS)
    else:
        (out_ref,) = rest[7:]
        out_ref[...] = out


def _post(mix_in, h, mix_w, mlp_w, ple_in, final_gain, tm):
    m, d = h.shape
    ssm_front = len(mix_in) == 3
    half_tiles = 0
    gm, wu, wd = mlp_w
    p, gp, wg, wp = ple_in
    ff, pd = wu.shape[1], p.shape[1]
    rows = lambda width: pl.BlockSpec((tm, width), lambda i: (i, 0))
    if ssm_front:
        y, z, gw = mix_in
        di = y.shape[1]
        ins = [y, z, h, gw, mix_w]
        specs = [rows(di), rows(di), rows(d), _resident((1, di)), _resident((di, d))]
    elif len(mix_in) == 2:
        o_lo, o_hi = mix_in
        nb, th, _ = o_lo.shape
        ht = half_tiles = th // tm
        lo = pl.BlockSpec((tm, d), lambda i: ((i // (2 * ht)) * ht + jnp.minimum(i % (2 * ht), ht - 1), 0))
        hi = pl.BlockSpec((tm, d), lambda i: ((i // (2 * ht)) * ht + jnp.maximum(i % (2 * ht) - ht, 0), 0))
        ins = [o_lo.reshape(nb * th, d), o_hi.reshape(nb * th, d), h, mix_w]
        specs = [lo, hi, rows(d), _resident((d, d))]
    else:
        ins = [mix_in[0], h, mix_w]
        specs = [rows(d), rows(d), _resident((d, d))]
    ins += [gm, wu, wd, p, gp, wg, wp]
    specs += [_resident((1, d)), _resident((d, ff)), _resident((ff, d)), rows(pd), _resident((1, d)),
              _resident((d, d)), _resident((pd, d))]
    final = final_gain is not None
    if final:
        ins.append(final_gain)
        specs.append(_resident((1, d)))
    return pl.pallas_call(
        functools.partial(_post_kernel, ssm_front=ssm_front, half_tiles=half_tiles, final=final, fc=min(ff, 1024)),
        grid=(m // tm,),
        in_specs=specs, out_specs=rows(d), out_shape=jax.ShapeDtypeStruct((m, d), F32),
        compiler_params=_params("parallel"), name="post_ssm" if ssm_front else "post_attn",
    )(*ins)


def _inproj_kernel(x_ref, g_ref, w_ref, z_ref, xbc_ref, dt_ref):
    xn = _rms(x_ref[...], g_ref[...], EPS).astype(BF16)
    nz, nx = z_ref.shape[1], xbc_ref.shape[1]
    z_ref[...] = jnp.dot(xn, w_ref[:, :nz], preferred_element_type=F32)
    xbc_ref[...] = jnp.dot(xn, w_ref[:, nz:nz + nx], preferred_element_type=F32)
    dt_ref[...] = jnp.dot(xn, w_ref[:, nz + nx:], preferred_element_type=F32)


def _inproj(x, gain, w, d_inner, conv_dim, tm):
    m, d = x.shape
    n = w.shape[1]
    ndt = n - d_inner - conv_dim
    row = lambda width: pl.BlockSpec((tm, width), lambda i: (i, 0))
    return pl.pallas_call(
        _inproj_kernel, grid=(m // tm,),
        in_specs=[row(d), _resident((1, d)), _resident((d, n))],
        out_specs=[row(d_inner), row(conv_dim), row(ndt)],
        out_shape=[jax.ShapeDtypeStruct((m, d_inner), F32), jax.ShapeDtypeStruct((m, conv_dim), F32),
                   jax.ShapeDtypeStruct((m, ndt), F32)],
        compiler_params=_params("parallel"), name="inproj",
    )(x, gain, w)


def _inproj_conv_kernel(x_ref, g_ref, w_ref, cw_ref, cb_ref, z_ref, xs_ref, bc_ref, dt_ref, tail_ref, hist_s,
                        *, tiles_per_seq, cblk):
    i = pl.program_id(0)
    tm = x_ref.shape[0]
    nz, nx, nbc = z_ref.shape[1], xs_ref.shape[1], bc_ref.shape[1]
    cd = nx + nbc
    xn = _rms(x_ref[...], g_ref[...], EPS).astype(BF16)

    @pl.when(i % tiles_per_seq == 0)
    def _():
        hist_s[...] = jnp.zeros(hist_s.shape, F32)

    nblk = cd // cblk
    zb = [(nz // LANES * k // nblk) * LANES for k in range(nblk + 1)]
    for k in range(nblk):
        c0 = k * cblk
        cols = slice(c0, c0 + cblk)
        raw = jnp.dot(xn, w_ref[:, nz + c0:nz + c0 + cblk], preferred_element_type=F32)
        if zb[k + 1] > zb[k]:
            z_ref[:, zb[k]:zb[k + 1]] = jnp.dot(xn, w_ref[:, zb[k]:zb[k + 1]], preferred_element_type=F32)
        act = _conv_silu(jnp.concatenate([hist_s[:, cols], raw], axis=0), cw_ref, cb_ref, cols)
        hist_s[:, cols] = raw[tm - SUBLANES:, :]
        if c0 < nx:
            xs_ref[:, cols] = act
        else:
            bc_ref[:, c0 - nx:c0 - nx + cblk] = act
    dt_ref[...] = jnp.dot(xn, w_ref[:, nz + cd:], preferred_element_type=F32)
    tail_ref[0] = hist_s[...]


def _inproj_conv(x, gain, w, cw, cb, d_inner, t, tm):
    m, d = x.shape
    n, cd = w.shape[1], cw.shape[1]
    nbc, ndt = cd - d_inner, n - d_inner - cd
    cblk = 512
    assert d_inner % cblk == 0 and nbc % cblk == 0 and t % tm == 0
    tps = t // tm
    row = lambda width: pl.BlockSpec((tm, width), lambda i: (i, 0))
    sds = lambda width: jax.ShapeDtypeStruct((m, width), F32)
    return pl.pallas_call(
        functools.partial(_inproj_conv_kernel, tiles_per_seq=tps, cblk=cblk), grid=(m // tm,),
        in_specs=[row(d), _resident((1, d)), _resident((d, n)), _resident((CONV_K, cd)), _resident((1, cd))],
        out_specs=[row(d_inner), row(d_inner), row(nbc), row(ndt),
                   pl.BlockSpec((1, SUBLANES, cd), lambda i: (i // tps, 0, 0))],
        out_shape=[sds(d_inner), sds(d_inner), sds(nbc), sds(ndt), jax.ShapeDtypeStruct((m // t, SUBLANES, cd), F32)],
        scratch_shapes=[pltpu.VMEM((SUBLANES, cd), F32)],
        compiler_params=_params("arbitrary"), name="inproj_conv",
    )(x, gain, w, cw, cb)


def _softplus(x):
    return jnp.maximum(x, 0.0) + jnp.log1p(jnp.exp(-jnp.abs(x)))


def _conv_silu(w, cw_ref, cb_ref, cols):
    conv = cb_ref[:, cols] + cw_ref[CONV_K - 1:CONV_K, cols] * w[SUBLANES:]
    for s in range(1, CONV_K):
        tap = CONV_K - 1 - s
        conv = conv + cw_ref[tap:tap + 1, cols] * pltpu.roll(w, s, 0)[SUBLANES:]
    return conv * jax.nn.sigmoid(conv)


def _ssd_conv_kernel(xbc_ref, dt_ref, c0_ref, h0_ref, cw_ref, cb_ref, dtb_ref, alog_ref, dsk_ref,
                     y_ref, hout_ref, ht_s, win_s, act_s, *, valid, n_heads):
    L = xbc_ref.shape[1]
    di = n_heads * SSM_HEAD_DIM

    @pl.when(pl.program_id(1) == 0)
    def _():
        win_s[0:SUBLANES, :] = c0_ref[0]

    win_s[SUBLANES:SUBLANES + L, :] = xbc_ref[0]
    cblk = 4 * LANES
    for c0 in range(0, act_s.shape[1], cblk):
        cols = slice(c0, c0 + cblk)
        act_s[:, cols] = _conv_silu(win_s[0:L + SUBLANES, cols], cw_ref, cb_ref, cols)
    win_s[0:SUBLANES, :] = win_s[L:L + SUBLANES, :]
    _ssd_chunk(lambda cols: act_s[:, cols], lambda s: act_s[:, di + s:di + s + D_STATE],
               dt_ref, h0_ref, dtb_ref, alog_ref, dsk_ref, y_ref, hout_ref, ht_s, valid=valid, n_heads=n_heads)


def _ssd_act_kernel(xs_ref, bc_ref, dt_ref, h0_ref, dtb_ref, alog_ref, dsk_ref, y_ref, hout_ref, ht_s,
                    *, valid, n_heads):
    _ssd_chunk(lambda cols: xs_ref[0, :, cols], lambda s: bc_ref[0, :, s:s + D_STATE],
               dt_ref, h0_ref, dtb_ref, alog_ref, dsk_ref, y_ref, hout_ref, ht_s, valid=valid, n_heads=n_heads)


def _ssd_chunk(x_of, bc_of, dt_ref, h0_ref, dtb_ref, alog_ref, dsk_ref, y_ref, hout_ref, ht_s, *, valid, n_heads):
    c = pl.program_id(1)
    nchunks = pl.num_programs(1)
    L = dt_ref.shape[1]
    hpg = n_heads // SSM_GROUPS
    gw = hpg * SSM_HEAD_DIM

    @pl.when(c == 0)
    def _():
        ht_s[...] = h0_ref[0].T

    dt = _softplus(dt_ref[0] + dtb_ref[...])
    lane = lax.broadcasted_iota(jnp.int32, dt.shape, 1)
    ok = lane < n_heads
    if valid < L:
        ok = ok & (lax.broadcasted_iota(jnp.int32, dt.shape, 0) < valid)
    dt = jnp.where(ok, dt, 0.0)
    dta = dt * (-jnp.exp(alog_ref[...]))
    tril = lax.broadcasted_iota(jnp.int32, (L, L), 0) >= lax.broadcasted_iota(jnp.int32, (L, L), 1)
    trilb = jnp.where(tril, 1.0, 0.0).astype(BF16)
    hi = dta.astype(BF16)
    r1 = dta - hi.astype(F32)
    mid = r1.astype(BF16)
    lo = (r1 - mid.astype(F32)).astype(BF16)
    acs = (jnp.dot(trilb, hi, preferred_element_type=F32) + jnp.dot(trilb, mid, preferred_element_type=F32)
           + jnp.dot(trilb, lo, preferred_element_type=F32))
    acs_t = acs.T
    last = acs[L - 1:L, :]
    pw = 2 * SSM_HEAD_DIM
    assert L == pw
    first =lax.broadcasted_iota(jnp.int32, (L, pw), 1) < SSM_HEAD_DIM

    def pair(a, h0, rows):
        return jnp.where(first[:rows], jnp.broadcast_to(a[:, h0:h0 + 1], (rows, pw)),
                         jnp.broadcast_to(a[:, h0 + 1:h0 + 2], (rows, pw)))

    for g in range(SSM_GROUPS):
        bg_t = bc_of(g * D_STATE).T.astype(BF16)
        cg = bc_of((SSM_GROUPS + g) * D_STATE).astype(BF16)
        cbm = jnp.dot(cg, bg_t, preferred_element_type=F32)
        hg = ht_s[:, g * gw:(g + 1) * gw]
        yin = jnp.dot(cg, hg.astype(BF16), preferred_element_type=F32)
        xw, el = [], []
        for pr in range(hpg // 2):
            h0 = g * hpg + 2 * pr
            cols = slice(h0 * SSM_HEAD_DIM, (h0 + 2) * SSM_HEAD_DIM)
            x2 = x_of(cols)
            acols = [jnp.broadcast_to(acs[:, h0 + r:h0 + r + 1], (L, L)) for r in range(2)]
            acol = jnp.where(first, acols[0], acols[1])
            last2 = pair(last, h0, 1)
            xdt = x2 * pair(dt, h0, L)
            xdtb = xdt.astype(BF16)
            ys = []
            for r in range(2):
                dec = jnp.exp(jnp.where(tril, acols[r] - acs_t[h0 + r:h0 + r + 1, :], NEG_INF))
                ys.append(jnp.dot((cbm * dec).astype(BF16), xdtb, preferred_element_type=F32))
            y_ref[0, :, cols] = (jnp.where(first, ys[0], ys[1]) + yin[:, pr * pw:(pr + 1) * pw] * jnp.exp(acol)
                                 + dsk_ref[:, cols] * x2)
            xw.append((xdt * jnp.exp(last2 - acol)).astype(BF16))
            el.append(jnp.exp(last2))
        xw = jnp.concatenate(xw, axis=1) if len(xw) > 1 else xw[0]
        el = jnp.concatenate(el, axis=1) if len(el) > 1 else el[0]
        ht_s[:, g * gw:(g + 1) * gw] = hg * el + jnp.dot(bg_t, xw, preferred_element_type=F32)

    @pl.when(c == nchunks - 1)
    def _():
        hout_ref[0] = ht_s[...].T


def _ssd(x_in, dt_raw, h0, dtb, alog, dsk, valid, conv=None):
    b, t, _ = dt_raw.shape
    di = h0.shape[1]
    L = min(CHUNK, t)
    chunk = lambda width: pl.BlockSpec((1, L, width), lambda bi, c: (bi, c, 0))
    per_b = lambda rows, width: pl.BlockSpec((1, rows, width), lambda bi, c: (bi, 0, 0))
    small = lambda shape: pl.BlockSpec(shape, lambda bi, c: (0,) * len(shape))
    tail_specs = [small((1, LANES)), small((1, LANES)), small((1, di))]
    scratch = [pltpu.VMEM((D_STATE, di), F32)]
    if conv is not None:
        (xbc,), (c0, cw, cb) = x_in, conv
        cd = xbc.shape[2]
        body = _ssd_conv_kernel
        ins = [xbc, dt_raw, c0, h0, cw, cb, dtb, alog, dsk]
        specs = [chunk(cd), chunk(LANES), per_b(SUBLANES, cd), per_b(di, D_STATE), small((CONV_K, cd)),
                 small((1, cd))] + tail_specs
        scratch += [pltpu.VMEM((L + 2 * SUBLANES, cd), F32), pltpu.VMEM((L, cd), F32)]
    else:
        xs, bc = x_in
        body = _ssd_act_kernel
        ins = [xs, bc, dt_raw, h0, dtb, alog, dsk]
        specs = [chunk(di), chunk(bc.shape[2]), chunk(LANES), per_b(di, D_STATE)] + tail_specs
    return pl.pallas_call(
        functools.partial(body, valid=valid, n_heads=di // SSM_HEAD_DIM), grid=(b, t // L),
        in_specs=specs, out_specs=[chunk(di), per_b(di, D_STATE)],
        out_shape=[jax.ShapeDtypeStruct((b, t, di), F32), jax.ShapeDtypeStruct((b, di, D_STATE), F32)],
        scratch_shapes=scratch,
        compiler_params=_params("parallel", "arbitrary"), name="ssd",
    )(*ins)


def _tile(m, want):
    return want if m % want == 0 else m


def kernel(x_prompt, x_sample, cache_k, cache_v, state_ssm, state_conv, page_table, p_prompt, p_sample, attn_norm, w_qkv, lambda_q1, lambda_k1, lambda_q2, lambda_k2, subln_w, w_o, ssm_norm, w_in, conv_w, conv_b, dt_bias, a_log, d_skip, gnorm_w, w_out, mlp_norm, w_up, w_down, ple_norm, w_ple_gate, w_ple_proj, final_norm):
    bp, tp, d = x_prompt.shape
    bs, ts, _ = x_sample.shape
    depth = mlp_norm.shape[0]
    past_len = page_table.shape[1] * cache_k.shape[2]
    n_heads_ssm = dt_bias.shape[1] if dt_bias.ndim == 2 else 0
    mp, msz = bp * tp, bs * ts
    tm_p = _tile(mp, 512)
    tm_s = msz
    bf = lambda a: a.astype(BF16)
    row = lambda a: a.reshape(1, -1)

    hp = x_prompt.reshape(mp, d)
    hs = x_sample.reshape(msz, d)
    tabs_p = _rope_tables(jnp.arange(tp, dtype=jnp.int32))
    tabs_s = _rope_tables(jnp.tile(past_len + jnp.arange(ts, dtype=jnp.int32), bs))

    kp_l, vp_l, ks_l, vs_l, sp_l, cp_l, ss_l, cs_l = [], [], [], [], [], [], [], []
    for i in range(depth):
        mlp_w = (row(mlp_norm[i]), bf(w_up[i]), bf(w_down[i]))
        ple_w = (row(ple_norm[i]), bf(w_ple_gate[i]), bf(w_ple_proj[i]))
        ple_p = (p_prompt[i].reshape(mp, -1),) + ple_w
        ple_s = (p_sample[i].reshape(msz, -1),) + ple_w
        fin = row(final_norm) if i == depth - 1 else None
        if i % 2 == 0:
            a = i // 2
            lam_init = 0.8 - 0.6 * math.exp(-0.3 * i)
            lam_params = [row(lambda_q1[a]), row(lambda_k1[a]), row(lambda_q2[a]), row(lambda_k2[a])]
            wqkv, wo, gain = bf(w_qkv[a]), bf(w_o[a]), row(attn_norm[a])
            q, kt, v = _qkv(hp, gain, wqkv, tabs_p, _tile(tp, 512), True)
            o = _prompt_attention(q, kt, v.reshape(bp, tp, d), lam_params,
                                  subln_w[a].reshape(V_DIM, 1), lam_init, _tile(tp, 512))
            hp = _post(tuple(o), hp, wo, mlp_w, ple_p, fin, _tile(tp // 2, 512))
            kp_l.append(jnp.transpose(kt.reshape(bp, d // HEAD_DIM, HEAD_DIM, tp), (0, 3, 1, 2)))
            vp_l.append(v.reshape(bp, tp, d // V_DIM, V_DIM))
            q, k, v = _qkv(hs, gain, wqkv, tabs_s, tm_s, False)
            o = _sample_attention(q.reshape(bs, ts, d), k.reshape(bs, ts, d), v.reshape(bs, ts, d), cache_k, cache_v,
                                  a, page_table, lam_params, row(subln_w[a]), lam_init, 16)
            hs = _post((o.reshape(msz, d),), hs, wo, mlp_w, ple_s, fin, tm_s)
            ks_l.append(k.reshape(bs, ts, d // HEAD_DIM, HEAD_DIM))
            vs_l.append(v.reshape(bs, ts, d // V_DIM, V_DIM))
        else:
            s = i // 2
            nh = n_heads_ssm
            di = nh * SSM_HEAD_DIM
            cd = conv_w.shape[2]
            pad_h = LANES - nh
            win = bf(jnp.pad(w_in[s], ((0, 0), (0, pad_h))))
            dtb = jnp.pad(dt_bias[s], (0, pad_h)).reshape(1, LANES)
            alog = jnp.pad(a_log[s], (0, pad_h)).reshape(1, LANES)
            dsk = jnp.repeat(d_skip[s], SSM_HEAD_DIM).reshape(1, di)
            gain, gw, wout = row(ssm_norm[s]), row(gnorm_w[s]), bf(w_out[s])
            cw, cb = conv_w[s], row(conv_b[s])
            z, xs, bca, dtr, tail = _inproj_conv(hp, gain, win, cw, cb, di, tp, _tile(tp, 256))
            y, st = _ssd((xs.reshape(bp, tp, di), bca.reshape(bp, tp, cd - di)), dtr.reshape(bp, tp, LANES),
                         jnp.zeros((bp, di, D_STATE), F32), dtb, alog, dsk, CHUNK)
            hp = _post((y.reshape(mp, di), z, gw), hp, wout, mlp_w, ple_p, fin, _tile(mp, 512))
            sp_l.append(st.reshape(bp, nh, SSM_HEAD_DIM, D_STATE))
            cp_l.append(tail[:, SUBLANES - (CONV_K - 1):])
            z, xbc, dtr = _inproj(hs, gain, win, di, cd, tm_s)
            xbc3 = xbc.reshape(bs, ts, cd)
            padt = ((0, 0), (0, CHUNK - ts), (0, 0))
            c0 = jnp.pad(state_conv[s], ((0, 0), (SUBLANES - (CONV_K - 1), 0), (0, 0)))
            y, st = _ssd((jnp.pad(xbc3, padt),), jnp.pad(dtr.reshape(bs, ts, LANES), padt),
                         state_ssm[s].reshape(bs, di, D_STATE), dtb, alog, dsk, ts, conv=(c0, cw, cb))
            hs = _post((y[:, :ts].reshape(msz, di), z, gw), hs, wout, mlp_w, ple_s, fin, tm_s)
            ss_l.append(st.reshape(bs, nh, SSM_HEAD_DIM, D_STATE))
            cs_l.append(jnp.concatenate([state_conv[s], xbc3], axis=1)[:, ts:])

    return (hp.reshape(bp, tp, d), hs.reshape(bs, ts, d), jnp.stack(kp_l), jnp.stack(vp_l), jnp.stack(ks_l),
            jnp.stack(vs_l), jnp.stack(sp_l), jnp.stack(cp_l), jnp.stack(ss_l), jnp.stack(cs_l))
```

```python
import functools
import math

import jax
import jax.numpy as jnp
from jax import lax
from jax.experimental import pallas as pl
from jax.experimental.pallas import tpu as pltpu

F32 = jnp.float32
BF16 = jnp.bfloat16

HEAD_DIM = 64
V_DIM = 2 * HEAD_DIM
ROT_DIM = HEAD_DIM // 4
ROPE_THETA = 500000.0
SUBLN_EPS = 1e-5
EPS = 1e-6
NEG_INF = -1e30
LOG2E = 1.4426950408889634
SSM_HEAD_DIM = 64
SSM_GROUPS = 8
D_STATE = 128
CONV_K = 4
CHUNK = 128
LANES = 128
SUBLANES = 8
VMEM_LIMIT = 56 * 1024 * 1024


def _params(*sem):
    return pltpu.CompilerParams(dimension_semantics=sem, vmem_limit_bytes=VMEM_LIMIT)


def _resident(shape):
    return pl.BlockSpec(shape, lambda *_: (0,) * len(shape), pipeline_mode=pl.Buffered(1))


def _rms(x, g, eps):
    ms = jnp.mean(x * x, axis=-1, keepdims=True)
    return x * lax.rsqrt(ms + eps) * g


def _diff_lambda(lq1, lk1, lq2, lk2, lam_init):
    s1 = jnp.sum(lq1[...] * lk1[...], axis=1, keepdims=True)
    s2 = jnp.sum(lq2[...] * lk2[...], axis=1, keepdims=True)
    return jnp.exp(s1) - jnp.exp(s2) + lam_init


def _qkv_kernel(x_ref, g_ref, w_ref, c_ref, s1_ref, s2_ref, q_ref, k_ref, v_ref, *, k_transposed):
    d = x_ref.shape[1]
    xn = _rms(x_ref[...], g_ref[...], EPS).astype(BF16)
    c, s1, s2 = c_ref[...], s1_ref[...], s2_ref[...]
    for part in range(2):
        y = jnp.dot(xn, w_ref[:, part * d:(part + 1) * d], preferred_element_type=F32)
        for hb in range(d // LANES):
            cols = slice(hb * LANES, (hb + 1) * LANES)
            blk = y[:, cols]
            rot = blk * c + pltpu.roll(blk, LANES - ROT_DIM // 2, 1) * s1 + pltpu.roll(blk, ROT_DIM // 2, 1) * s2
            if not k_transposed:
                (q_ref, k_ref)[part][:, cols] = rot
            elif part == 0:
                q_ref[0, cols, :] = (rot * (HEAD_DIM ** -0.5 * LOG2E)).T.astype(q_ref.dtype)
            else:
                k_ref[0, cols, :] = rot.T
    v_ref[...] = jnp.dot(xn, w_ref[:, 2 * d:3 * d], preferred_element_type=F32)


def _rope_tables(pos):
    half = ROT_DIM // 2
    inv_freq = jnp.power(ROPE_THETA, -jnp.arange(half, dtype=F32) * (2.0 / ROT_DIM))
    ang = pos.astype(F32)[:, None] * inv_freq[None, :]
    cos, sin = jnp.cos(ang), jnp.sin(ang)
    t = pos.shape[0]
    z8 = jnp.zeros((t, half), F32)
    rest0 = jnp.zeros((t, HEAD_DIM - ROT_DIM), F32)
    c = jnp.concatenate([cos, cos, jnp.ones((t, HEAD_DIM - ROT_DIM), F32)], axis=1)
    s1 = jnp.concatenate([-sin, z8, rest0], axis=1)
    s2 = jnp.concatenate([z8, sin, rest0], axis=1)
    rep = LANES // HEAD_DIM
    return tuple(jnp.tile(a, (1, rep)) for a in (c, s1, s2))


def _qkv(x, gain, w, tabs, tm, k_transposed):
    m, d = x.shape
    t = tabs[0].shape[0]
    nt = t // tm
    row = pl.BlockSpec((tm, d), lambda i: (i, 0))
    tab = pl.BlockSpec((tm, LANES), lambda i: (i % nt, 0))
    qspec = kspec = row
    qshape = kshape = jax.ShapeDtypeStruct((m, d), F32)
    if k_transposed:
        qspec = kspec = pl.BlockSpec((1, d, tm), lambda i: (i // nt, 0, i % nt))
        qshape = jax.ShapeDtypeStruct((m // t, d, t), BF16)
        kshape = jax.ShapeDtypeStruct((m // t, d, t), F32)
    return pl.pallas_call(
        functools.partial(_qkv_kernel, k_transposed=k_transposed),
        grid=(m // tm,),
        in_specs=[row, _resident((1, d)), _resident((d, 3 * d)), tab, tab, tab],
        out_specs=[qspec, kspec, row],
        out_shape=[qshape, kshape, jax.ShapeDtypeStruct((m, d), F32)],
        compiler_params=_params("parallel"),
        name="qkv",
    )(x, gain, w, *tabs)


def _attn_kernel(qlo_ref, qhi_ref, k_ref, v_ref, lq1, lk1, lq2, lk2, sw_ref, olo_ref, ohi_ref,
                 kb, vtb, qbd_s, sa, sb, mxa, mxb, m_s, l_s, acc_s, *, tq, lam_init):
    p = pl.program_id(2)
    nt = vtb.shape[0]
    tiles = (p, nt - 1 - p)

    @pl.when(p == 0)
    def _():
        for c in range(nt):
            kb[c * tq:(c + 1) * tq, :] = k_ref[0, :, c * tq:(c + 1) * tq].T.astype(BF16)
            vtb[c] = v_ref[0, c * tq:(c + 1) * tq, :].T.astype(BF16)

    for sel, q_ref in enumerate((qlo_ref, qhi_ref)):
        qt = q_ref[0]
        rowi = lax.broadcasted_iota(jnp.int32, qt.shape, 0)
        zero = jnp.zeros_like(qt)
        qbd_s[sel] = jnp.concatenate([jnp.where(rowi < HEAD_DIM, qt, zero), jnp.where(rowi >= HEAD_DIM, qt, zero)],
                                     axis=1)
    m_s[...] = jnp.full(m_s.shape, NEG_INF, F32)
    l_s[...] = jnp.zeros(l_s.shape, F32)
    acc_s[...] = jnp.zeros(acc_s.shape, F32)

    def scores(sel, blk, masked, s_ref, mx_ref):
        start = pl.multiple_of(blk * tq, tq)
        st = jnp.dot(kb[pl.ds(start, tq), :], qbd_s[sel], preferred_element_type=F32)
        if masked:
            kj = lax.broadcasted_iota(jnp.int32, st.shape, 0)
            qc = lax.broadcasted_iota(jnp.int32, st.shape, 1)
            qc = jnp.where(qc >= tq, qc - tq, qc)
            st = jnp.where(kj <= qc, st, NEG_INF)
        s_ref[...] = st
        mx_ref[...] = jnp.max(st, axis=0, keepdims=True)

    def consume(sel, blk, s_ref, mx_ref):
        m_prev = m_s[sel]
        m_new = jnp.maximum(m_prev, mx_ref[...])
        alpha = jnp.exp2(m_prev - m_new)
        pt = jnp.exp2(s_ref[...] - m_new)
        l_s[sel] = alpha * l_s[sel] + jnp.sum(pt, axis=0, keepdims=True)
        acc_s[sel] = alpha * acc_s[sel] + jnp.dot(vtb[blk], pt.astype(BF16), preferred_element_type=F32)
        m_s[sel] = m_new

    lam = _diff_lambda(lq1, lk1, lq2, lk2, lam_init)

    def finalize(sel, o_ref):
        a = acc_s[sel] * (1.0 / l_s[sel])
        ot = a[:, :tq] - lam * a[:, tq:]
        ms = jnp.mean(ot * ot, axis=0, keepdims=True)
        ot = ot * lax.rsqrt(ms + SUBLN_EPS) * sw_ref[...] * (1.0 - lam_init)
        o_ref[0] = ot.T.astype(o_ref.dtype)

    items = [(0, tiles[0], True), (1, tiles[1], True)]
    for idx in range(nt - 1):
        hi = (idx >= p).astype(jnp.int32)
        items.append((hi, idx - hi * p, False))
    bufs = ((sa, mxa), (sb, mxb))
    scores(*items[0], *bufs[0])
    for j, (sel, blk, _) in enumerate(items):
        if j + 1 < len(items):
            scores(*items[j + 1], *bufs[(j + 1) % 2])
        consume(sel, blk, *bufs[j % 2])
    finalize(0, olo_ref)
    finalize(1, ohi_ref)


def _prompt_attention(q, k, v, lam_params, subln_col, lam_init, tq):
    b, t, d = v.shape
    hp = d // V_DIM
    nt = t // tq
    assert nt % 2 == 0
    half = nt // 2
    qlo = pl.BlockSpec((1, V_DIM, tq), lambda bi, h, p: (bi, h, p))
    qhi = pl.BlockSpec((1, V_DIM, tq), lambda bi, h, p: (bi, h, nt - 1 - p))
    olo = pl.BlockSpec((1, tq, V_DIM), lambda bi, h, p: (bi, p, h))
    ohi = pl.BlockSpec((1, tq, V_DIM), lambda bi, h, p: (bi, half - 1 - p, h))
    kspec = pl.BlockSpec((1, V_DIM, t), lambda bi, h, p: (bi, h, 0))
    vspec = pl.BlockSpec((1, t, V_DIM), lambda bi, h, p: (bi, 0, h))
    lspec = _resident((1, HEAD_DIM))
    stat = pltpu.VMEM((2, 1, 2 * tq), F32)
    return pl.pallas_call(
        functools.partial(_attn_kernel, tq=tq, lam_init=lam_init),
        grid=(b, hp, half),
        in_specs=[qlo, qhi, kspec, vspec, lspec, lspec, lspec, lspec, _resident((V_DIM, 1))],
        out_specs=[olo, ohi],
        out_shape=[jax.ShapeDtypeStruct((b, t // 2, d), BF16)] * 2,
        scratch_shapes=[pltpu.VMEM((t, V_DIM), BF16), pltpu.VMEM((nt, V_DIM, tq), BF16),
                        pltpu.VMEM((2, V_DIM, 2 * tq), BF16),
                        pltpu.VMEM((tq, 2 * tq), F32), pltpu.VMEM((tq, 2 * tq), F32),
                        pltpu.VMEM((1, 2 * tq), F32), pltpu.VMEM((1, 2 * tq), F32),
                        stat, stat, pltpu.VMEM((2, V_DIM, 2 * tq), F32)],
        compiler_params=_params("parallel", "parallel", "arbitrary"),
        name="prompt_attn",
    )(q, q, k, v, *lam_params, subln_col)


def _sattn_kernel(pt_ref, q_ref, kn_ref, vn_ref, *rest, pps, lam_init, n_new):
    del pt_ref
    k_refs, v_refs = rest[:pps], rest[pps:2 * pps]
    lq1, lk1, lq2, lk2, sw_ref, o_ref, m_s, l_s, acc_s = rest[2 * pps:]
    g = pl.program_id(1)
    ng = pl.num_programs(1)
    d = q_ref.shape[2]
    nc = d // HEAD_DIM
    nh = d // V_DIM
    page = kn_ref.shape[3]
    rows = nc * SUBLANES

    @pl.when(g == 0)
    def _():
        m_s[...] = jnp.full(m_s.shape, NEG_INF, F32)
        l_s[...] = jnp.zeros(l_s.shape, F32)
        acc_s[...] = jnp.zeros(acc_s.shape, F32)

    q = (q_ref[0] * (HEAD_DIM ** -0.5)).astype(BF16)
    qcs = [q[:, c * HEAD_DIM:(c + 1) * HEAD_DIM] for c in range(nc)]

    def update(kv_pairs, mask):
        scores = []
        for kr, _ in kv_pairs:
            s = jnp.concatenate([jnp.dot(qcs[c], kr[c].astype(BF16), preferred_element_type=F32) for c in range(nc)],
                                axis=0)
            if mask is not None:
                s = jnp.where(mask, s, NEG_INF)
            scores.append(s)
        mx = scores[0]
        for s in scores[1:]:
            mx = jnp.maximum(mx, s)
        m_prev = m_s[...]
        m_new = jnp.maximum(m_prev, jnp.max(mx, axis=1, keepdims=True))
        alpha = jnp.exp(m_prev - m_new)
        psum = None
        pv = [None] * nh
        for s, (_, vr) in zip(scores, kv_pairs):
            p = jnp.exp(s - m_new)
            psum = p if psum is None else psum + p
            pb = p.astype(BF16)
            for h in range(nh):
                vh = vr[pl.ds(h, page, stride=nh), :].astype(BF16)
                t = jnp.dot(pb[2 * h * SUBLANES:(2 * h + 2) * SUBLANES, :], vh, preferred_element_type=F32)
                pv[h] = t if pv[h] is None else pv[h] + t
        acc_s[...] = alpha * acc_s[...] + jnp.concatenate(pv, axis=0)
        l_s[...] = alpha * l_s[...] + jnp.sum(psum, axis=1, keepdims=True)
        m_s[...] = m_new

    update(list(zip(k_refs, v_refs)), None)

    @pl.when(g == ng - 1)
    def _():
        kj = lax.broadcasted_iota(jnp.int32, (rows, page), 1)
        slot = lax.broadcasted_iota(jnp.int32, (rows, page), 0) % SUBLANES
        mask = (kj < n_new) & (kj <= slot)
        update([(kn_ref.at[0], vn_ref.at[0])], mask)
        a = acc_s[...] * (1.0 / l_s[...])
        lam = _diff_lambda(lq1, lk1, lq2, lk2, lam_init)
        for h in range(nh):
            a1 = a[(2 * h) * SUBLANES:(2 * h + 1) * SUBLANES, :]
            a2 = a[(2 * h + 1) * SUBLANES:(2 * h + 2) * SUBLANES, :]
            o = a1 - lam * a2
            o_ref[0, :, h * V_DIM:(h + 1) * V_DIM] = _rms(o, sw_ref[...], SUBLN_EPS) * (1.0 - lam_init)


def _sample_attention(q, k_new, v_new, cache_k, cache_v, layer, page_table, lam_params, subln_row, lam_init, pps):
    s, n_new, d = q.shape
    na, n_pool, page, nc, _ = cache_k.shape
    nh = cache_v.shape[3]
    n_pages = page_table.shape[1]
    rows = nc * SUBLANES
    qp = jnp.pad(q, ((0, 0), (0, SUBLANES - n_new), (0, 0)))
    ck = jnp.transpose(cache_k, (0, 1, 3, 4, 2))
    cv = cache_v.reshape(na, n_pool, page * nh, V_DIM)
    knp = jnp.pad(jnp.transpose(k_new.reshape(s, n_new, nc, HEAD_DIM), (0, 2, 3, 1)),
                  ((0, 0), (0, 0), (0, 0), (0, page - n_new)))
    vnp = jnp.pad(v_new, ((0, 0), (0, page - n_new), (0, 0))).reshape(s, page * nh, V_DIM)

    kspec = lambda j: pl.BlockSpec((None, None, nc, HEAD_DIM, page),
                                   lambda si, g, pt: (layer, pt[si, g * pps + j], 0, 0, 0))
    vspec = lambda j: pl.BlockSpec((None, None, page * nh, V_DIM), lambda si, g, pt: (layer, pt[si, g * pps + j], 0, 0))
    seq8 = pl.BlockSpec((1, SUBLANES, d), lambda si, g, pt: (si, 0, 0))
    knspec = pl.BlockSpec((1, nc, HEAD_DIM, page), lambda si, g, pt: (si, 0, 0, 0))
    vnspec = pl.BlockSpec((1, page * nh, V_DIM), lambda si, g, pt: (si, 0, 0))
    small = lambda shape: pl.BlockSpec(shape, lambda si, g, pt: (0,) * len(shape))

    out = pl.pallas_call(
        functools.partial(_sattn_kernel, pps=pps, lam_init=lam_init, n_new=n_new),
        grid_spec=pltpu.PrefetchScalarGridSpec(
            num_scalar_prefetch=1,
            grid=(s, n_pages // pps),
            in_specs=[seq8, knspec, vnspec] + [kspec(j) for j in range(pps)] + [vspec(j) for j in range(pps)]
            + [small((1, HEAD_DIM))] * 4 + [small((1, V_DIM))],
            out_specs=seq8,
            scratch_shapes=[pltpu.VMEM((rows, 1), F32), pltpu.VMEM((rows, 1), F32), pltpu.VMEM((rows, V_DIM), F32)],
        ),
        out_shape=jax.ShapeDtypeStruct((s, SUBLANES, d), F32),
        compiler_params=_params("parallel", "arbitrary"),
        name="sample_attn",
    )(page_table, qp, knp, vnp, *([ck] * pps), *([cv] * pps), *lam_params, subln_row)
    return out[:, :n_new]


def _post_kernel(*refs, ssm_front, half_tiles, final, fc):
    if ssm_front:
        y_ref, z_ref, h_ref, gw_ref, wo_ref = refs[:5]
        rest = refs[5:]
        z = z_ref[...]
        yg = y_ref[...] * (z * jax.nn.sigmoid(z))
        gs = yg.shape[1] // SSM_GROUPS
        mix = jnp.concatenate([_rms(yg[:, g * gs:(g + 1) * gs], gw_ref[:, g * gs:(g + 1) * gs], EPS).astype(BF16)
                               for g in range(SSM_GROUPS)], axis=1)
    elif half_tiles:
        olo_ref, ohi_ref, h_ref, wo_ref = refs[:4]
        rest = refs[4:]
        first_half = pl.program_id(0) % (2 * half_tiles) < half_tiles
        mix = jnp.where(first_half, olo_ref[...], ohi_ref[...]).astype(BF16)
    else:
        o_ref, h_ref, wo_ref = refs[:3]
        rest = refs[3:]
        mix = o_ref[...].astype(BF16)
    gm_ref, wu_ref, wd_ref, p_ref, gp_ref, wg_ref, wp_ref = rest[:7]
    h = h_ref[...] + jnp.dot(mix, wo_ref[...], preferred_element_type=F32)
    xn = _rms(h, gm_ref[...], EPS).astype(BF16)
    for c in range(wu_ref.shape[1] // fc):
        a = jnp.maximum(jnp.dot(xn, wu_ref[:, c * fc:(c + 1) * fc], preferred_element_type=F32), 0.0)
        h = h + jnp.dot((a * a).astype(BF16), wd_ref[c * fc:(c + 1) * fc, :], preferred_element_type=F32)
    gate = jax.nn.sigmoid(jnp.dot(_rms(h, gp_ref[...], EPS).astype(BF16), wg_ref[...], preferred_element_type=F32))
    out = h + gate * jnp.dot(p_ref[...].astype(BF16), wp_ref[...], preferred_element_type=F32)
    if final:
        fn_ref, out_ref = rest[7:]
        out_ref[...] = _rms(out, fn_ref[...], EPS)
    else:
        (out_ref,) = rest[7:]
        out_ref[...] = out


def _post(mix_in, h, mix_w, mlp_w, ple_in, final_gain, tm):
    m, d = h.shape
    ssm_front = len(mix_in) == 3
    half_tiles = 0
    gm, wu, wd = mlp_w
    p, gp, wg, wp = ple_in
    ff, pd = wu.shape[1], p.shape[1]
    rows = lambda width: pl.BlockSpec((tm, width), lambda i: (i, 0))
    if ssm_front:
        y, z, gw = mix_in
        di = y.shape[1]
        ins = [y, z, h, gw, mix_w]
        specs = [rows(di), rows(di), rows(d), _resident((1, di)), _resident((di, d))]
    elif len(mix_in) == 2:
        o_lo, o_hi = mix_in
        nb, th, _ = o_lo.shape
        ht = half_tiles = th // tm
        lo = pl.BlockSpec((tm, d), lambda i: ((i // (2 * ht)) * ht + jnp.minimum(i % (2 * ht), ht - 1), 0))
        hi = pl.BlockSpec((tm, d), lambda i: ((i // (2 * ht)) * ht + jnp.maximum(i % (2 * ht) - ht, 0), 0))
        ins = [o_lo.reshape(nb * th, d), o_hi.reshape(nb * th, d), h, mix_w]
        specs = [lo, hi, rows(d), _resident((d, d))]
    else:
        ins = [mix_in[0], h, mix_w]
        specs = [rows(d), rows(d), _resident((d, d))]
    ins += [gm, wu, wd, p, gp, wg, wp]
    specs += [_resident((1, d)), _resident((d, ff)), _resident((ff, d)), rows(pd), _resident((1, d)),
              _resident((d, d)), _resident((pd, d))]
    final = final_gain is not None
    if final:
        ins.append(final_gain)
        specs.append(_resident((1, d)))
    return pl.pallas_call(
        functools.partial(_post_kernel, ssm_front=ssm_front, half_tiles=half_tiles, final=final, fc=min(ff, 1024)),
        grid=(m // tm,),
        in_specs=specs, out_specs=rows(d), out_shape=jax.ShapeDtypeStruct((m, d), F32),
        compiler_params=_params("parallel"), name="post_ssm" if ssm_front else "post_attn",
    )(*ins)


def _inproj_kernel(x_ref, g_ref, w_ref, z_ref, xbc_ref, dt_ref):
    xn = _rms(x_ref[...], g_ref[...], EPS).astype(BF16)
    nz, nx = z_ref.shape[1], xbc_ref.shape[1]
    z_ref[...] = jnp.dot(xn, w_ref[:, :nz], preferred_element_type=F32)
    xbc_ref[...] = jnp.dot(xn, w_ref[:, nz:nz + nx], preferred_element_type=F32)
    dt_ref[...] = jnp.dot(xn, w_ref[:, nz + nx:], preferred_element_type=F32)


def _inproj(x, gain, w, d_inner, conv_dim, tm):
    m, d = x.shape
    n = w.shape[1]
    ndt = n - d_inner - conv_dim
    row = lambda width: pl.BlockSpec((tm, width), lambda i: (i, 0))
    return pl.pallas_call(
        _inproj_kernel, grid=(m // tm,),
        in_specs=[row(d), _resident((1, d)), _resident((d, n))],
        out_specs=[row(d_inner), row(conv_dim), row(ndt)],
        out_shape=[jax.ShapeDtypeStruct((m, d_inner), F32), jax.ShapeDtypeStruct((m, conv_dim), F32),
                   jax.ShapeDtypeStruct((m, ndt), F32)],
        compiler_params=_params("parallel"), name="inproj",
    )(x, gain, w)


def _inproj_conv_kernel(x_ref, g_ref, w_ref, cw_ref, cb_ref, z_ref, xs_ref, bc_ref, dt_ref, tail_ref, hist_s,
                        *, tiles_per_seq, cblk):
    i = pl.program_id(0)
    tm = x_ref.shape[0]
    nz, nx, nbc = z_ref.shape[1], xs_ref.shape[1], bc_ref.shape[1]
    cd = nx + nbc
    xn = _rms(x_ref[...], g_ref[...], EPS).astype(BF16)

    @pl.when(i % tiles_per_seq == 0)
    def _():
        hist_s[...] = jnp.zeros(hist_s.shape, F32)

    nblk = cd // cblk
    zb = [(nz // LANES * k // nblk) * LANES for k in range(nblk + 1)]
    for k in range(nblk):
        c0 = k * cblk
        cols = slice(c0, c0 + cblk)
        raw = jnp.dot(xn, w_ref[:, nz + c0:nz + c0 + cblk], preferred_element_type=F32)
        if zb[k + 1] > zb[k]:
            z_ref[:, zb[k]:zb[k + 1]] = jnp.dot(xn, w_ref[:, zb[k]:zb[k + 1]], preferred_element_type=F32)
        act = _conv_silu(jnp.concatenate([hist_s[:, cols], raw], axis=0), cw_ref, cb_ref, cols)
        hist_s[:, cols] = raw[tm - SUBLANES:, :]
        if c0 < nx:
            xs_ref[:, cols] = act
        else:
            bc_ref[:, c0 - nx:c0 - nx + cblk] = act
    dt_ref[...] = jnp.dot(xn, w_ref[:, nz + cd:], preferred_element_type=F32)
    tail_ref[0] = hist_s[...]


def _inproj_conv(x, gain, w, cw, cb, d_inner, t, tm):
    m, d = x.shape
    n, cd = w.shape[1], cw.shape[1]
    nbc, ndt = cd - d_inner, n - d_inner - cd
    cblk = 512
    assert d_inner % cblk == 0 and nbc % cblk == 0 and t % tm == 0
    tps = t // tm
    row = lambda width: pl.BlockSpec((tm, width), lambda i: (i, 0))
    sds = lambda width: jax.ShapeDtypeStruct((m, width), F32)
    return pl.pallas_call(
        functools.partial(_inproj_conv_kernel, tiles_per_seq=tps, cblk=cblk), grid=(m // tm,),
        in_specs=[row(d), _resident((1, d)), _resident((d, n)), _resident((CONV_K, cd)), _resident((1, cd))],
        out_specs=[row(d_inner), row(d_inner), row(nbc), row(ndt),
                   pl.BlockSpec((1, SUBLANES, cd), lambda i: (i // tps, 0, 0))],
        out_shape=[sds(d_inner), sds(d_inner), sds(nbc), sds(ndt), jax.ShapeDtypeStruct((m // t, SUBLANES, cd), F32)],
        scratch_shapes=[pltpu.VMEM((SUBLANES, cd), F32)],
        compiler_params=_params("arbitrary"), name="inproj_conv",
    )(x, gain, w, cw, cb)


def _softplus(x):
    return jnp.maximum(x, 0.0) + jnp.log1p(jnp.exp(-jnp.abs(x)))


def _conv_silu(w, cw_ref, cb_ref, cols):
    conv = cb_ref[:, cols] + cw_ref[CONV_K - 1:CONV_K, cols] * w[SUBLANES:]
    for s in range(1, CONV_K):
        tap = CONV_K - 1 - s
        conv = conv + cw_ref[tap:tap + 1, cols] * pltpu.roll(w, s, 0)[SUBLANES:]
    return conv * jax.nn.sigmoid(conv)


def _ssd_conv_kernel(xbc_ref, dt_ref, c0_ref, h0_ref, cw_ref, cb_ref, dtb_ref, alog_ref, dsk_ref,
                     y_ref, hout_ref, ht_s, win_s, act_s, *, valid, n_heads):
    L = xbc_ref.shape[1]
    di = n_heads * SSM_HEAD_DIM

    @pl.when(pl.program_id(1) == 0)
    def _():
        win_s[0:SUBLANES, :] = c0_ref[0]

    win_s[SUBLANES:SUBLANES + L, :] = xbc_ref[0]
    cblk = 4 * LANES
    for c0 in range(0, act_s.shape[1], cblk):
        cols = slice(c0, c0 + cblk)
        act_s[:, cols] = _conv_silu(win_s[0:L + SUBLANES, cols], cw_ref, cb_ref, cols)
    win_s[0:SUBLANES, :] = win_s[L:L + SUBLANES, :]
    _ssd_chunk(lambda cols: act_s[:, cols], lambda s: act_s[:, di + s:di + s + D_STATE],
               dt_ref, h0_ref, dtb_ref, alog_ref, dsk_ref, y_ref, hout_ref, ht_s, valid=valid, n_heads=n_heads)


def _ssd_act_kernel(xs_ref, bc_ref, dt_ref, h0_ref, dtb_ref, alog_ref, dsk_ref, y_ref, hout_ref, ht_s,
                    *, valid, n_heads):
    _ssd_chunk(lambda cols: xs_ref[0, :, cols], lambda s: bc_ref[0, :, s:s + D_STATE],
               dt_ref, h0_ref, dtb_ref, alog_ref, dsk_ref, y_ref, hout_ref, ht_s, valid=valid, n_heads=n_heads)


def _ssd_chunk(x_of, bc_of, dt_ref, h0_ref, dtb_ref, alog_ref, dsk_ref, y_ref, hout_ref, ht_s, *, valid, n_heads):
    c = pl.program_id(1)
    nchunks = pl.num_programs(1)
    L = dt_ref.shape[1]
    hpg = n_heads // SSM_GROUPS
    gw = hpg * SSM_HEAD_DIM

    @pl.when(c == 0)
    def _():
        ht_s[...] = h0_ref[0].T

    dt = _softplus(dt_ref[0] + dtb_ref[...])
    lane = lax.broadcasted_iota(jnp.int32, dt.shape, 1)
    ok = lane < n_heads
    if valid < L:
        ok = ok & (lax.broadcasted_iota(jnp.int32, dt.shape, 0) < valid)
    dt = jnp.where(ok, dt, 0.0)
    dta = dt * (-jnp.exp(alog_ref[...]))
    tril = lax.broadcasted_iota(jnp.int32, (L, L), 0) >= lax.broadcasted_iota(jnp.int32, (L, L), 1)
    trilb = jnp.where(tril, 1.0, 0.0).astype(BF16)
    hi = dta.astype(BF16)
    r1 = dta - hi.astype(F32)
    mid = r1.astype(BF16)
    lo = (r1 - mid.astype(F32)).astype(BF16)
    acs = (jnp.dot(trilb, hi, preferred_element_type=F32) + jnp.dot(trilb, mid, preferred_element_type=F32)
           + jnp.dot(trilb, lo, preferred_element_type=F32))
    acs_t = acs.T
    last = acs[L - 1:L, :]
    pw = 2 * SSM_HEAD_DIM
    assert L == pw
    first =lax.broadcasted_iota(jnp.int32, (L, pw), 1) < SSM_HEAD_DIM

    def pair(a, h0, rows):
        return jnp.where(first[:rows], jnp.broadcast_to(a[:, h0:h0 + 1], (rows, pw)),
                         jnp.broadcast_to(a[:, h0 + 1:h0 + 2], (rows, pw)))

    for g in range(SSM_GROUPS):
        bg_t = bc_of(g * D_STATE).T.astype(BF16)
        cg = bc_of((SSM_GROUPS + g) * D_STATE).astype(BF16)
        cbm = jnp.dot(cg, bg_t, preferred_element_type=F32)
        hg = ht_s[:, g * gw:(g + 1) * gw]
        yin = jnp.dot(cg, hg.astype(BF16), preferred_element_type=F32)
        xw, el = [], []
        for pr in range(hpg // 2):
            h0 = g * hpg + 2 * pr
            cols = slice(h0 * SSM_HEAD_DIM, (h0 + 2) * SSM_HEAD_DIM)
            x2 = x_of(cols)
            acols = [jnp.broadcast_to(acs[:, h0 + r:h0 + r + 1], (L, L)) for r in range(2)]
            acol = jnp.where(first, acols[0], acols[1])
            last2 = pair(last, h0, 1)
            xdt = x2 * pair(dt, h0, L)
            xdtb = xdt.astype(BF16)
            ys = []
            for r in range(2):
                dec = jnp.exp(jnp.where(tril, acols[r] - acs_t[h0 + r:h0 + r + 1, :], NEG_INF))
                ys.append(jnp.dot((cbm * dec).astype(BF16), xdtb, preferred_element_type=F32))
            y_ref[0, :, cols] = (jnp.where(first, ys[0], ys[1]) + yin[:, pr * pw:(pr + 1) * pw] * jnp.exp(acol)
                                 + dsk_ref[:, cols] * x2)
            xw.append((xdt * jnp.exp(last2 - acol)).astype(BF16))
            el.append(jnp.exp(last2))
        xw = jnp.concatenate(xw, axis=1) if len(xw) > 1 else xw[0]
        el = jnp.concatenate(el, axis=1) if len(el) > 1 else el[0]
        ht_s[:, g * gw:(g + 1) * gw] = hg * el + jnp.dot(bg_t, xw, preferred_element_type=F32)

    @pl.when(c == nchunks - 1)
    def _():
        hout_ref[0] = ht_s[...].T


def _ssd(x_in, dt_raw, h0, dtb, alog, dsk, valid, conv=None):
    b, t, _ = dt_raw.shape
    di = h0.shape[1]
    L = min(CHUNK, t)
    chunk = lambda width: pl.BlockSpec((1, L, width), lambda bi, c: (bi, c, 0))
    per_b = lambda rows, width: pl.BlockSpec((1, rows, width), lambda bi, c: (bi, 0, 0))
    small = lambda shape: pl.BlockSpec(shape, lambda bi, c: (0,) * len(shape))
    tail_specs = [small((1, LANES)), small((1, LANES)), small((1, di))]
    scratch = [pltpu.VMEM((D_STATE, di), F32)]
    if conv is not None:
        (xbc,), (c0, cw, cb) = x_in, conv
        cd = xbc.shape[2]
        body = _ssd_conv_kernel
        ins = [xbc, dt_raw, c0, h0, cw, cb, dtb, alog, dsk]
        specs = [chunk(cd), chunk(LANES), per_b(SUBLANES, cd), per_b(di, D_STATE), small((CONV_K, cd)),
                 small((1, cd))] + tail_specs
        scratch += [pltpu.VMEM((L + 2 * SUBLANES, cd), F32), pltpu.VMEM((L, cd), F32)]
    else:
        xs, bc = x_in
        body = _ssd_act_kernel
        ins = [xs, bc, dt_raw, h0, dtb, alog, dsk]
        specs = [chunk(di), chunk(bc.shape[2]), chunk(LANES), per_b(di, D_STATE)] + tail_specs
    return pl.pallas_call(
        functools.partial(body, valid=valid, n_heads=di // SSM_HEAD_DIM), grid=(b, t // L),
        in_specs=specs, out_specs=[chunk(di), per_b(di, D_STATE)],
        out_shape=[jax.ShapeDtypeStruct((b, t, di), F32), jax.ShapeDtypeStruct((b, di, D_STATE), F32)],
        scratch_shapes=scratch,
        compiler_params=_params("parallel", "arbitrary"), name="ssd",
    )(*ins)


TM_QKV = 1024
TQ_ATTN = 512
TM_POST = 512
TM_INPROJ = 256
PAGES_PER_STEP = 16


def _tile(m, want):
    return want if m % want == 0 else m


def kernel(x_prompt, x_sample, cache_k, cache_v, state_ssm, state_conv, page_table, p_prompt, p_sample, attn_norm, w_qkv, lambda_q1, lambda_k1, lambda_q2, lambda_k2, subln_w, w_o, ssm_norm, w_in, conv_w, conv_b, dt_bias, a_log, d_skip, gnorm_w, w_out, mlp_norm, w_up, w_down, ple_norm, w_ple_gate, w_ple_proj, final_norm):
    bp, tp, d = x_prompt.shape
    bs, ts, _ = x_sample.shape
    depth = mlp_norm.shape[0]
    past_len = page_table.shape[1] * cache_k.shape[2]
    n_heads_ssm = dt_bias.shape[1] if dt_bias.ndim == 2 else 0
    mp, msz = bp * tp, bs * ts
    tm_s = msz
    bf = lambda a: a.astype(BF16)
    row = lambda a: a.reshape(1, -1)

    hp = x_prompt.reshape(mp, d)
    hs = x_sample.reshape(msz, d)
    tabs_p = _rope_tables(jnp.arange(tp, dtype=jnp.int32))
    tabs_s = _rope_tables(jnp.tile(past_len + jnp.arange(ts, dtype=jnp.int32), bs))

    kp_l, vp_l, ks_l, vs_l, sp_l, cp_l, ss_l, cs_l = [], [], [], [], [], [], [], []
    for i in range(depth):
        mlp_w = (row(mlp_norm[i]), bf(w_up[i]), bf(w_down[i]))
        ple_w = (row(ple_norm[i]), bf(w_ple_gate[i]), bf(w_ple_proj[i]))
        ple_p = (p_prompt[i].reshape(mp, -1),) + ple_w
        ple_s = (p_sample[i].reshape(msz, -1),) + ple_w
        fin = row(final_norm) if i == depth - 1 else None
        if i % 2 == 0:
            a = i // 2
            lam_init = 0.8 - 0.6 * math.exp(-0.3 * i)
            lam_params = [row(lambda_q1[a]), row(lambda_k1[a]), row(lambda_q2[a]), row(lambda_k2[a])]
            wqkv, wo, gain = bf(w_qkv[a]), bf(w_o[a]), row(attn_norm[a])
            q, kt, v = _qkv(hp, gain, wqkv, tabs_p, _tile(tp, TM_QKV), True)
            o = _prompt_attention(q, kt, v.reshape(bp, tp, d), lam_params,
                                  subln_w[a].reshape(V_DIM, 1), lam_init, _tile(tp, TQ_ATTN))
            hp = _post(tuple(o), hp, wo, mlp_w, ple_p, fin, _tile(tp // 2, TM_POST))
            kp_l.append(jnp.transpose(kt.reshape(bp, d // HEAD_DIM, HEAD_DIM, tp), (0, 3, 1, 2)))
            vp_l.append(v.reshape(bp, tp, d // V_DIM, V_DIM))
            q, k, v = _qkv(hs, gain, wqkv, tabs_s, tm_s, False)
            o = _sample_attention(q.reshape(bs, ts, d), k.reshape(bs, ts, d), v.reshape(bs, ts, d), cache_k, cache_v,
                                  a, page_table, lam_params, row(subln_w[a]), lam_init, PAGES_PER_STEP)
            hs = _post((o.reshape(msz, d),), hs, wo, mlp_w, ple_s, fin, tm_s)
            ks_l.append(k.reshape(bs, ts, d // HEAD_DIM, HEAD_DIM))
            vs_l.append(v.reshape(bs, ts, d // V_DIM, V_DIM))
        else:
            s = i // 2
            nh = n_heads_ssm
            di = nh * SSM_HEAD_DIM
            cd = conv_w.shape[2]
            pad_h = LANES - nh
            win = bf(jnp.pad(w_in[s], ((0, 0), (0, pad_h))))
            dtb = jnp.pad(dt_bias[s], (0, pad_h)).reshape(1, LANES)
            alog = jnp.pad(a_log[s], (0, pad_h)).reshape(1, LANES)
            dsk = jnp.repeat(d_skip[s], SSM_HEAD_DIM).reshape(1, di)
            gain, gw, wout = row(ssm_norm[s]), row(gnorm_w[s]), bf(w_out[s])
            cw, cb = conv_w[s], row(conv_b[s])
            z, xs, bca, dtr, tail = _inproj_conv(hp, gain, win, cw, cb, di, tp, _tile(tp, TM_INPROJ))
            y, st = _ssd((xs.reshape(bp, tp, di), bca.reshape(bp, tp, cd - di)), dtr.reshape(bp, tp, LANES),
                         jnp.zeros((bp, di, D_STATE), F32), dtb, alog, dsk, CHUNK)
            hp = _post((y.reshape(mp, di), z, gw), hp, wout, mlp_w, ple_p, fin, _tile(mp, TM_POST))
            sp_l.append(st.reshape(bp, nh, SSM_HEAD_DIM, D_STATE))
            cp_l.append(tail[:, SUBLANES - (CONV_K - 1):])
            z, xbc, dtr = _inproj(hs, gain, win, di, cd, tm_s)
            xbc3 = xbc.reshape(bs, ts, cd)
            padt = ((0, 0), (0, CHUNK - ts), (0, 0))
            c0 = jnp.pad(state_conv[s], ((0, 0), (SUBLANES - (CONV_K - 1), 0), (0, 0)))
            y, st = _ssd((jnp.pad(xbc3, padt),), jnp.pad(dtr.reshape(bs, ts, LANES), padt),
                         state_ssm[s].reshape(bs, di, D_STATE), dtb, alog, dsk, ts, conv=(c0, cw, cb))
            hs = _post((y[:, :ts].reshape(msz, di), z, gw), hs, wout, mlp_w, ple_s, fin, tm_s)
            ss_l.append(st.reshape(bs, nh, SSM_HEAD_DIM, D_STATE))
            cs_l.append(jnp.concatenate([state_conv[s], xbc3], axis=1)[:, ts:])

    return (hp.reshape(bp, tp, d), hs.reshape(bs, ts, d), jnp.stack(kp_l), jnp.stack(vp_l), jnp.stack(ks_l),
            jnp.stack(vs_l), jnp.stack(sp_l), jnp.stack(cp_l), jnp.stack(ss_l), jnp.stack(cs_l))
```
